```python
import math
import jax
import jax.numpy as jnp
from jax import lax
import numpy as np

D_MODEL = 1024
BATCH = 8
SEQ = 2048
DEPTH = 2
DEC_BATCH = 32
DEC_SEQ = 8
PAST_LEN = 16384
PAGE_SIZE = 128

N_MIXERS = 4
GROUP_WIDTH = D_MODEL // N_MIXERS
HEAD_DIM = 64
ROPE_THETA = 10000.0
RMS_EPS = 1e-6
GLA_HEADS = GROUP_WIDTH // HEAD_DIM
GLA_DV = HEAD_DIM
GLA_DK = HEAD_DIM // 2
GLA_GATE_RANK = 16
GLA_TAU = 16.0
GLA_CHUNK = 32
NSA_HEADS = GROUP_WIDTH // HEAD_DIM
NSA_BRANCHES = 3
NSA_KV_SLOTS = 6
CMP_BLOCK = 32
CMP_STRIDE = 16
CMP_HIDDEN = 128
SEL_BLOCK = 64
SEL_TOPK = 16
WINDOW = 512
NSA_Q_CHUNK = 64
WIN_Q_CHUNK = 128
FORCE_SCORE = 1e4
S5_GROUP_SIZE = 16
S5_GROUPS = GROUP_WIDTH // S5_GROUP_SIZE
S5_STATE = 64
S5_DT_MIN = 0.001
S5_DT_MAX = 0.1
MOBA_HEADS = GROUP_WIDTH // HEAD_DIM
MOBA_BLOCK = 256
MOBA_TOPK = 3
MOBA_Q_CHUNK = 32
D_FF = 2816
PLE_DIM = 256

MIX_SPLITS = (GLA_HEADS * GLA_DK, GLA_HEADS * GLA_DK, GLA_HEADS * GLA_DV, GLA_GATE_RANK, GLA_HEADS * GLA_DV,
              NSA_HEADS * HEAD_DIM, NSA_KV_SLOTS * HEAD_DIM, NSA_BRANCHES * NSA_HEADS,
              GROUP_WIDTH,
              MOBA_HEADS * HEAD_DIM, MOBA_HEADS * HEAD_DIM, MOBA_HEADS * HEAD_DIM)
MIX_IN = sum(MIX_SPLITS)

kernel_name = 'hybrid_gla_nsa_s5_moba_decode_step'


def rmsnorm(x, g):
    xf = x.astype(jnp.float32)
    y = xf * lax.rsqrt(jnp.mean(xf * xf, axis=-1, keepdims=True) + RMS_EPS)
    return (y * g.astype(jnp.float32)).astype(x.dtype)


def swiglu(x, w_in, w_out):
    g, u = jnp.split(x @ w_in, 2, axis=-1)
    return (jax.nn.silu(g) * u) @ w_out


def rope(x, pos):
    half = x.shape[-1] // 2
    inv = ROPE_THETA ** (-jnp.arange(half, dtype=jnp.float32) / half)
    ang = pos.astype(jnp.float32)[:, None] * inv
    shape = (1, pos.shape[0]) + (1,) * (x.ndim - 3) + (half,)
    cos, sin = jnp.cos(ang).reshape(shape), jnp.sin(ang).reshape(shape)
    xf = x.astype(jnp.float32)
    x1, x2 = xf[..., :half], xf[..., half:]
    return jnp.concatenate([x1 * cos - x2 * sin, x2 * cos + x1 * sin], axis=-1).astype(x.dtype)


def masked_softmax(s, mask):
    s = jnp.where(mask, s.astype(jnp.float32), -jnp.inf)
    m = jnp.max(s, axis=-1, keepdims=True)
    m = jnp.where(jnp.isfinite(m), m, 0.0)
    e = jnp.where(mask, jnp.exp(s - m), 0.0)
    return e / jnp.maximum(jnp.sum(e, axis=-1, keepdims=True), 1e-30)


def to_chunks(x, c):
    b, t = x.shape[:2]
    return jnp.moveaxis(x.reshape((b, t // c, c) + x.shape[2:]), 1, 0)


def from_chunks(x):
    x = jnp.moveaxis(x, 0, 1)
    return x.reshape((x.shape[0], x.shape[1] * x.shape[2]) + x.shape[3:])


def gather_past(cache, layer, page_table):
    g = cache[layer, page_table]
    return g.reshape((g.shape[0], g.shape[1] * g.shape[2]) + g.shape[3:])


def gla_recurrence(q, k, v, log_a, s0, chunk):
    tri = jnp.tril(jnp.ones((chunk, chunk), dtype=bool))

    def step(S, xs):
        qc, kc, vc, gc = xs
        b = jnp.cumsum(gc, axis=1)
        o_inter = jnp.einsum('bchk,bhkv->bchv', qc * jnp.exp(b), S)
        diff = b[:, :, None] - b[:, None, :]
        decay = jnp.exp(jnp.where(tri[None, :, :, None, None], diff, -jnp.inf))
        att = jnp.einsum('bihk,bjhk,bijhk->bhij', qc, kc, decay)
        o_intra = jnp.einsum('bhij,bjhv->bihv', att, vc)
        b_last = b[:, -1]
        S = jnp.exp(b_last)[..., None] * S + jnp.einsum('bjhk,bjhv->bhkv', kc * jnp.exp(b_last[:, None] - b), vc)
        return S, o_inter + o_intra

    s_fin, o = lax.scan(step, s0, (to_chunks(q, chunk), to_chunks(k, chunk), to_chunks(v, chunk), to_chunks(log_a, chunk)))
    return from_chunks(o), s_fin


def nsa_compress(x_ctx, pe, w1, w2):
    b, L, dh = x_ctx.shape
    n_cmp = (L - CMP_BLOCK) // CMP_STRIDE + 1
    idx = jnp.arange(n_cmp)[:, None] * CMP_STRIDE + jnp.arange(CMP_BLOCK)[None, :]
    blk = (x_ctx[:, idx] + pe).reshape(b, n_cmp, CMP_BLOCK * dh)
    return jax.nn.gelu(blk @ w1) @ w2


def nsa_compressed_selected(q_raw, q_rot, kc, vc, ks_ctx, vs_ctx, pos, q_chunk):
    b, t, h, dh = q_raw.shape
    n_cmp = kc.shape[1]
    L = ks_ctx.shape[1]
    n_sel = -(-L // SEL_BLOCK)
    padn = n_sel * SEL_BLOCK - L
    ksb = jnp.pad(ks_ctx, ((0, 0), (0, padn), (0, 0))).reshape(b, n_sel, SEL_BLOCK, dh)
    vsb = jnp.pad(vs_ctx, ((0, 0), (0, padn), (0, 0))).reshape(b, n_sel, SEL_BLOCK, dh)
    cstart = jnp.arange(n_cmp) * CMP_STRIDE
    jstart = jnp.arange(n_sel) * SEL_BLOCK
    overlap = ((cstart[:, None] < jstart[None, :] + SEL_BLOCK) & (cstart[:, None] + CMP_BLOCK > jstart[None, :])).astype(jnp.float32)
    cend = cstart + CMP_BLOCK - 1
    k_sel = min(SEL_TOPK, n_sel)
    scale = dh ** -0.5
    blk_ids = jnp.arange(n_sel)
    bi = jnp.arange(b)[:, None, None]

    def chunk_fn(xs):
        qr, qs, p = xs
        qc_len = p.shape[0]
        sc = jnp.einsum('bqhd,bnd->bqhn', qr, kc) * scale
        pc = masked_softmax(sc, (cend[None, :] <= p[:, None])[None, :, None, :])
        o_c = jnp.einsum('bqhn,bnd->bqhd', pc, vc)
        imp = jnp.einsum('bqhn,ns->bqs', pc, overlap)
        cur = p // SEL_BLOCK
        forced = (blk_ids[None, :] == 0) | (blk_ids[None, :] == cur[:, None])
        past = blk_ids[None, :] < cur[:, None]
        imp = jnp.where(forced[None], FORCE_SCORE, jnp.where(past[None], imp, -jnp.inf))
        top_s, top_i = lax.top_k(imp, k_sel)
        valid = jnp.isfinite(top_s)
        kg = ksb[bi, top_i]
        vg = vsb[bi, top_i]
        kpos = top_i[..., None] * SEL_BLOCK + jnp.arange(SEL_BLOCK)
        m = valid[..., None] & (kpos <= p[None, :, None, None])
        ss = jnp.einsum('bqhd,bqksd->bqhks', qs, kg) * scale
        ps = masked_softmax(ss.reshape(b, qc_len, h, k_sel * SEL_BLOCK), m.reshape(b, qc_len, 1, k_sel * SEL_BLOCK))
        o_s = jnp.einsum('bqhn,bqnd->bqhd', ps, vg.reshape(b, qc_len, k_sel * SEL_BLOCK, dh))
        return o_c, o_s

    o_c, o_s = lax.map(chunk_fn, (to_chunks(q_raw, q_chunk), to_chunks(q_rot, q_chunk), pos.reshape(-1, q_chunk)))
    return from_chunks(o_c), from_chunks(o_s)


def window_attention(q, k_new, v_new, k_buf, v_buf, pos0, q_chunk):
    b, t, h, dh = q.shape
    pad = WINDOW - k_buf.shape[1]
    zeros = jnp.zeros((b, pad, dh), k_new.dtype)
    k_ext = jnp.concatenate([zeros, k_buf, k_new], axis=1)
    v_ext = jnp.concatenate([zeros, v_buf, v_new], axis=1)
    n_c = t // q_chunk
    idx = jnp.arange(n_c)[:, None] * q_chunk + jnp.arange(WINDOW + q_chunk)[None, :]
    kb, vb = k_ext[:, idx], v_ext[:, idx]
    qb = q.reshape(b, n_c, q_chunk, h, dh)
    qpos = pos0 + jnp.arange(t).reshape(n_c, q_chunk)
    kpos = pos0 - WINDOW + idx
    diff = qpos[:, :, None] - kpos[:, None, :]
    mask = (diff >= 0) & (diff < WINDOW) & (kpos[:, None, :] >= 0)
    s = jnp.einsum('bcqhd,bckd->bchqk', qb, kb) * dh ** -0.5
    p = masked_softmax(s, mask[None, :, None])
    return jnp.einsum('bchqk,bckd->bcqhd', p, vb).reshape(b, t, h, dh)


def complex_affine_combine(e1, e2):
    a1r, a1i, b1r, b1i = e1
    a2r, a2i, b2r, b2i = e2
    return (a2r * a1r - a2i * a1i, a2r * a1i + a2i * a1r,
            a2r * b1r - a2i * b1i + b2r, a2r * b1i + a2i * b1r + b2i)


def s5_mixer(u, h0, a_re, a_im, b_re, b_im, c_re, c_im, d, log_dt, glu_w, glu_b):
    f32 = jnp.float32
    b, t, _ = u.shape
    uf = u.astype(f32)
    ug = uf.reshape(b, t, S5_GROUPS, S5_GROUP_SIZE)
    ar, ai = a_re.astype(f32), a_im.astype(f32)
    dt = jnp.exp(log_dt.astype(f32))[:, None]
    mag = jnp.exp(ar * dt)
    abar_re, abar_im = mag * jnp.cos(ai * dt), mag * jnp.sin(ai * dt)
    den = ar * ar + ai * ai
    fr = ((abar_re - 1.0) * ar + abar_im * ai) / den
    fi = (abar_im * ar - (abar_re - 1.0) * ai) / den
    br, bim = b_re.astype(f32), b_im.astype(f32)
    bbar_re = fr[..., None] * br - fi[..., None] * bim
    bbar_im = fr[..., None] * bim + fi[..., None] * br
    bu_re = jnp.einsum('gnc,btgc->btgn', bbar_re, ug)
    bu_im = jnp.einsum('gnc,btgc->btgn', bbar_im, ug)
    if h0 is not None:
        h_re0, h_im0 = h0[0].astype(f32), h0[1].astype(f32)
        bu_re = bu_re.at[:, 0].add(abar_re * h_re0 - abar_im * h_im0)
        bu_im = bu_im.at[:, 0].add(abar_re * h_im0 + abar_im * h_re0)
    a_full_re = jnp.broadcast_to(abar_re, bu_re.shape)
    a_full_im = jnp.broadcast_to(abar_im, bu_im.shape)
    _, _, h_re, h_im = lax.associative_scan(complex_affine_combine, (a_full_re, a_full_im, bu_re, bu_im), axis=1)
    y = jnp.einsum('gcn,btgn->btgc', c_re.astype(f32), h_re) - jnp.einsum('gcn,btgn->btgc', c_im.astype(f32), h_im)
    y = y.reshape(b, t, GROUP_WIDTH) + d.astype(f32) * uf
    g = jax.nn.gelu(y)
    out = g * jax.nn.sigmoid(g @ glu_w.astype(f32) + glu_b.astype(f32))
    return out, h_re[:, -1], h_im[:, -1]


def moba_attention(q, k_ctx, v_ctx, pos, q_chunk):
    b, t, h, dh = q.shape
    L = k_ctx.shape[1]
    n_full = L // MOBA_BLOCK
    n_blk = -(-L // MOBA_BLOCK)
    n_cols = max(n_full, MOBA_TOPK)
    means = k_ctx[:, :n_full * MOBA_BLOCK].reshape(b, n_full, MOBA_BLOCK, h, dh).astype(jnp.float32).mean(axis=2)
    means = jnp.pad(means, ((0, 0), (0, n_cols - n_full), (0, 0), (0, 0)))
    padn = n_blk * MOBA_BLOCK - L
    kb = jnp.pad(k_ctx, ((0, 0), (0, padn), (0, 0), (0, 0))).reshape(b, n_blk, MOBA_BLOCK, h, dh).transpose(0, 3, 1, 2, 4)
    vb = jnp.pad(v_ctx, ((0, 0), (0, padn), (0, 0), (0, 0))).reshape(b, n_blk, MOBA_BLOCK, h, dh).transpose(0, 3, 1, 2, 4)
    scale = dh ** -0.5
    cols = jnp.arange(n_cols)
    bi = jnp.arange(b)[:, None, None, None]
    hi = jnp.arange(h)[None, None, :, None]

    def chunk_fn(xs):
        qc, p = xs
        qc_len = p.shape[0]
        own = p // MOBA_BLOCK
        s = jnp.einsum('bqhd,bnhd->bqhn', qc.astype(jnp.float32), means)
        past = cols[None, :] < own[:, None]
        s = jnp.where(past[None, :, None, :], s, -jnp.inf)
        top_s, top_i = lax.top_k(s, MOBA_TOPK)
        valid = jnp.concatenate([jnp.isfinite(top_s), jnp.ones((b, qc_len, h, 1), bool)], axis=-1)
        own_b = jnp.broadcast_to(own[None, :, None, None], (b, qc_len, h, 1)).astype(top_i.dtype)
        idx = jnp.minimum(jnp.concatenate([top_i, own_b], axis=-1), n_blk - 1)
        kg = kb[bi, hi, idx]
        vg = vb[bi, hi, idx]
        kpos = idx[..., None] * MOBA_BLOCK + jnp.arange(MOBA_BLOCK)
        m = valid[..., None] & (kpos <= p[None, :, None, None, None])
        sc = jnp.einsum('bqhd,bqhjsd->bqhjs', qc, kg) * scale
        n_keys = (MOBA_TOPK + 1) * MOBA_BLOCK
        pr = masked_softmax(sc.reshape(b, qc_len, h, n_keys), m.reshape(b, qc_len, h, n_keys))
        return jnp.einsum('bqhn,bqhnd->bqhd', pr, vg.reshape(b, qc_len, h, n_keys, dh))

    o = lax.map(chunk_fn, (to_chunks(q, q_chunk), pos.reshape(-1, q_chunk)))
    return from_chunks(o)


def token_mixing(xn, pos0, lp, past):
    f32 = jnp.float32
    b, t, _ = xn.shape
    prompt = past is None
    pos = pos0 + jnp.arange(t, dtype=jnp.int32)
    cuts = np.cumsum(MIX_SPLITS)[:-1].tolist()
    (g_q, g_k, g_v, g_z, g_og, n_q, n_kv, n_g, s_u, m_q, m_k, m_v) = jnp.split(xn @ lp['w_mix_in'], cuts, axis=-1)

    gq = g_q.reshape(b, t, GLA_HEADS, GLA_DK).astype(f32) * GLA_DK ** -0.5
    gk = g_k.reshape(b, t, GLA_HEADS, GLA_DK).astype(f32)
    gv = g_v.reshape(b, t, GLA_HEADS, GLA_DV).astype(f32)
    log_a = jax.nn.log_sigmoid((g_z @ lp['gla_gate_w'] + lp['gla_gate_b']).astype(f32)).reshape(b, t, GLA_HEADS, GLA_DK) / GLA_TAU
    s0 = jnp.zeros((b, GLA_HEADS, GLA_DK, GLA_DV), f32) if prompt else past['gla'].astype(f32)
    o_gla, gla_state = gla_recurrence(gq, gk, gv, log_a, s0, GLA_CHUNK if prompt else t)
    o_gla = (rmsnorm(o_gla, lp['gla_norm']) * jax.nn.silu(g_og.reshape(b, t, GLA_HEADS, GLA_DV))).reshape(b, t, GROUP_WIDTH)

    nq = n_q.reshape(b, t, NSA_HEADS, HEAD_DIM)
    nq_rot = rope(nq, pos)
    kv = n_kv.reshape(b, t, NSA_KV_SLOTS, HEAD_DIM)
    ks_new, kw_new = rope(kv[:, :, 2], pos), rope(kv[:, :, 4], pos)
    nsa_rows = jnp.stack([kv[:, :, 0], kv[:, :, 1], ks_new, kv[:, :, 3]], axis=2)
    win_rows = jnp.stack([kw_new, kv[:, :, 5]], axis=2)
    if prompt:
        nsa_ctx = nsa_rows
        win_buf = jnp.zeros((b, 0, 2, HEAD_DIM), win_rows.dtype)
        win_len = min(WINDOW, t)
    else:
        nsa_ctx = jnp.concatenate([past['nsa'], nsa_rows], axis=1)
        win_buf = past['win']
        win_len = win_buf.shape[1]
    kc_tok = nsa_compress(nsa_ctx[:, :, 0], lp['nsa_cmp_pe'][0], lp['nsa_cmp_w1'][0], lp['nsa_cmp_w2'][0])
    vc_tok = nsa_compress(nsa_ctx[:, :, 1], lp['nsa_cmp_pe'][1], lp['nsa_cmp_w1'][1], lp['nsa_cmp_w2'][1])
    o_cmp, o_slc = nsa_compressed_selected(nq, nq_rot, kc_tok, vc_tok, nsa_ctx[:, :, 2], nsa_ctx[:, :, 3], pos, NSA_Q_CHUNK if prompt else t)
    o_win = window_attention(nq_rot, win_rows[:, :, 0], win_rows[:, :, 1], win_buf[:, :, 0], win_buf[:, :, 1], pos0, WIN_Q_CHUNK if prompt else t)
    gates = jax.nn.sigmoid(n_g.astype(f32)).reshape(b, t, NSA_BRANCHES, NSA_HEADS, 1)
    o_nsa = (gates[:, :, 0] * o_cmp + gates[:, :, 1] * o_slc + gates[:, :, 2] * o_win).reshape(b, t, GROUP_WIDTH)
    new_win = jnp.concatenate([win_buf, win_rows], axis=1)[:, -win_len:]

    h0 = None if prompt else (past['s5_re'], past['s5_im'])
    o_s5, s5_re, s5_im = s5_mixer(s_u, h0, lp['s5_a_re'], lp['s5_a_im'], lp['s5_b_re'], lp['s5_b_im'], lp['s5_c_re'], lp['s5_c_im'],
                                  lp['s5_d'], lp['s5_log_dt'], lp['s5_glu_w'], lp['s5_glu_b'])

    mq = rope(m_q.reshape(b, t, MOBA_HEADS, HEAD_DIM), pos)
    mk = rope(m_k.reshape(b, t, MOBA_HEADS, HEAD_DIM), pos)
    moba_rows = jnp.stack([mk, m_v.reshape(b, t, MOBA_HEADS, HEAD_DIM)], axis=2)
    moba_ctx = moba_rows if prompt else jnp.concatenate([past['moba'], moba_rows], axis=1)
    o_moba = moba_attention(mq, moba_ctx[:, :, 0], moba_ctx[:, :, 1], pos, MOBA_Q_CHUNK if prompt else t).reshape(b, t, GROUP_WIDTH)

    out = jnp.concatenate([o_gla, o_nsa, o_s5, o_moba], axis=-1) @ lp['w_mix_out']
    new_state = {'nsa': nsa_rows, 'moba': moba_rows, 'win': new_win, 'gla': gla_state, 's5_re': s5_re, 's5_im': s5_im}
    return out, new_state


def trunk_layer(x, p_emb, pos0, lp, past):
    h = x + 0.5 * swiglu(rmsnorm(x, lp['norm_ffn1']), lp['ffn1_w_in'], lp['ffn1_w_out'])
    mix, st = token_mixing(rmsnorm(h, lp['norm_mix']), pos0, lp, past)
    h = h + mix
    h = h + 0.5 * swiglu(rmsnorm(h, lp['norm_ffn2']), lp['ffn2_w_in'], lp['ffn2_w_out'])
    gate = jax.nn.sigmoid((rmsnorm(h, lp['norm_ple']) @ lp['ple_gate_w']).astype(jnp.float32))
    h = h + (p_emb @ lp['ple_w']) * gate
    return h, st


def setup_inputs(seed: int = 0) -> dict:
    key = jax.random.key(seed)
    ks = iter(jax.random.split(key, 64))
    f32 = jnp.float32

    def nrm(shape, scale=1.0):
        return jax.random.normal(next(ks), shape, f32) * scale

    def gain(shape):
        return 1.0 + nrm(shape, 0.01)

    n_pages = PAST_LEN // PAGE_SIZE
    n_pool = (DEC_BATCH * n_pages * 5) // 4
    win_len = min(WINDOW, PAST_LEN)
    page_table = jax.random.permutation(next(ks), n_pool)[:DEC_BATCH * n_pages].reshape(DEC_BATCH, n_pages).astype(jnp.int32)
    a_im0 = jnp.pi * jnp.arange(S5_STATE, dtype=f32)
    return {
        'x_prompt': nrm((BATCH, SEQ, D_MODEL)),
        'x_sample': nrm((DEC_BATCH, DEC_SEQ, D_MODEL)),
        'p_prompt': nrm((DEPTH, BATCH, SEQ, PLE_DIM)),
        'p_sample': nrm((DEPTH, DEC_BATCH, DEC_SEQ, PLE_DIM)),
        'cache_nsa_kv': nrm((DEPTH, n_pool, PAGE_SIZE, 4, HEAD_DIM)),
        'cache_moba_kv': nrm((DEPTH, n_pool, PAGE_SIZE, 2, MOBA_HEADS, HEAD_DIM)),
        'state_nsa_win': nrm((DEPTH, DEC_BATCH, win_len, 2, HEAD_DIM)),
        'state_gla': nrm((DEPTH, DEC_BATCH, GLA_HEADS, GLA_DK, GLA_DV)),
        'state_s5_re': nrm((DEPTH, DEC_BATCH, S5_GROUPS, S5_STATE), 0.5),
        'state_s5_im': nrm((DEPTH, DEC_BATCH, S5_GROUPS, S5_STATE), 0.5),
        'page_table': page_table,
        'norm_ffn1': gain((DEPTH, D_MODEL)),
        'ffn1_w_in': nrm((DEPTH, D_MODEL, 2 * D_FF), D_MODEL ** -0.5),
        'ffn1_w_out': nrm((DEPTH, D_FF, D_MODEL), D_FF ** -0.5),
        'norm_mix': gain((DEPTH, D_MODEL)),
        'w_mix_in': nrm((DEPTH, D_MODEL, MIX_IN), D_MODEL ** -0.5),
        'w_mix_out': nrm((DEPTH, D_MODEL, D_MODEL), D_MODEL ** -0.5),
        'gla_gate_w': nrm((DEPTH, GLA_GATE_RANK, GLA_HEADS * GLA_DK), GLA_GATE_RANK ** -0.5),
        'gla_gate_b': nrm((DEPTH, GLA_HEADS * GLA_DK), 0.1),
        'gla_norm': gain((DEPTH, GLA_DV)),
        'nsa_cmp_pe': nrm((DEPTH, 2, CMP_BLOCK, HEAD_DIM), 0.02),
        'nsa_cmp_w1': nrm((DEPTH, 2, CMP_BLOCK * HEAD_DIM, CMP_HIDDEN), (CMP_BLOCK * HEAD_DIM) ** -0.5),
        'nsa_cmp_w2': nrm((DEPTH, 2, CMP_HIDDEN, HEAD_DIM), CMP_HIDDEN ** -0.5),
        's5_a_re': -0.5 + nrm((DEPTH, S5_GROUPS, S5_STATE), 0.01),
        's5_a_im': a_im0 + nrm((DEPTH, S5_GROUPS, S5_STATE), 0.01),
        's5_b_re': nrm((DEPTH, S5_GROUPS, S5_STATE, S5_GROUP_SIZE), (2 * S5_GROUP_SIZE) ** -0.5),
        's5_b_im': nrm((DEPTH, S5_GROUPS, S5_STATE, S5_GROUP_SIZE), (2 * S5_GROUP_SIZE) ** -0.5),
        's5_c_re': nrm((DEPTH, S5_GROUPS, S5_GROUP_SIZE, S5_STATE), (2 * S5_STATE) ** -0.5),
        's5_c_im': nrm((DEPTH, S5_GROUPS, S5_GROUP_SIZE, S5_STATE), (2 * S5_STATE) ** -0.5),
        's5_d': nrm((DEPTH, GROUP_WIDTH)),
        's5_log_dt': jax.random.uniform(next(ks), (DEPTH, S5_GROUPS), f32, math.log(S5_DT_MIN), math.log(S5_DT_MAX)),
        's5_glu_w': nrm((DEPTH, GROUP_WIDTH, GROUP_WIDTH), GROUP_WIDTH ** -0.5),
        's5_glu_b': nrm((DEPTH, GROUP_WIDTH), 0.01),
        'norm_ffn2': gain((DEPTH, D_MODEL)),
        'ffn2_w_in': nrm((DEPTH, D_MODEL, 2 * D_FF), D_MODEL ** -0.5),
        'ffn2_w_out': nrm((DEPTH, D_FF, D_MODEL), D_FF ** -0.5),
        'norm_ple': gain((DEPTH, D_MODEL)),
        'ple_w': nrm((DEPTH, PLE_DIM, D_MODEL), PLE_DIM ** -0.5),
        'ple_gate_w': nrm((DEPTH, D_MODEL, D_MODEL), D_MODEL ** -0.5),
        'norm_final': gain((D_MODEL,)),
    }


def reference(x_prompt, x_sample, p_prompt, p_sample, cache_nsa_kv, cache_moba_kv, state_nsa_win, state_gla,
              state_s5_re, state_s5_im, page_table, norm_ffn1, ffn1_w_in, ffn1_w_out, norm_mix, w_mix_in, w_mix_out,
              gla_gate_w, gla_gate_b, gla_norm, nsa_cmp_pe, nsa_cmp_w1, nsa_cmp_w2, s5_a_re, s5_a_im, s5_b_re, s5_b_im,
              s5_c_re, s5_c_im, s5_d, s5_log_dt, s5_glu_w, s5_glu_b, norm_ffn2, ffn2_w_in, ffn2_w_out, norm_ple, ple_w,
              ple_gate_w, norm_final):
    params = {'norm_ffn1': norm_ffn1, 'ffn1_w_in': ffn1_w_in, 'ffn1_w_out': ffn1_w_out, 'norm_mix': norm_mix,
              'w_mix_in': w_mix_in, 'w_mix_out': w_mix_out, 'gla_gate_w': gla_gate_w, 'gla_gate_b': gla_gate_b,
              'gla_norm': gla_norm, 'nsa_cmp_pe': nsa_cmp_pe, 'nsa_cmp_w1': nsa_cmp_w1, 'nsa_cmp_w2': nsa_cmp_w2,
              's5_a_re': s5_a_re, 's5_a_im': s5_a_im, 's5_b_re': s5_b_re, 's5_b_im': s5_b_im, 's5_c_re': s5_c_re,
              's5_c_im': s5_c_im, 's5_d': s5_d, 's5_log_dt': s5_log_dt, 's5_glu_w': s5_glu_w, 's5_glu_b': s5_glu_b,
              'norm_ffn2': norm_ffn2, 'ffn2_w_in': ffn2_w_in, 'ffn2_w_out': ffn2_w_out, 'norm_ple': norm_ple,
              'ple_w': ple_w, 'ple_gate_w': ple_gate_w}
    hp, hs = x_prompt, x_sample
    st_prompt, st_sample = [], []
    for i in range(DEPTH):
        lp = {name: w[i] for name, w in params.items()}
        hp, sp = trunk_layer(hp, p_prompt[i], 0, lp, None)
        past = {'nsa': gather_past(cache_nsa_kv, i, page_table), 'moba': gather_past(cache_moba_kv, i, page_table),
                'win': state_nsa_win[i], 'gla': state_gla[i], 's5_re': state_s5_re[i], 's5_im': state_s5_im[i]}
        hs, ss = trunk_layer(hs, p_sample[i], PAST_LEN, lp, past)
        st_prompt.append(sp)
        st_sample.append(ss)

    def stack(states, name):
        return jnp.stack([s[name] for s in states], axis=0)

    y_prompt = rmsnorm(hp, norm_final)
    y_sample = rmsnorm(hs, norm_final)
    return (y_prompt, y_sample,
            stack(st_prompt, 'nsa'), stack(st_sample, 'nsa'),
            stack(st_prompt, 'moba'), stack(st_sample, 'moba'),
            stack(st_prompt, 'win'), stack(st_sample, 'win'),
            stack(st_prompt, 'gla'), stack(st_sample, 'gla'),
            stack(st_prompt, 's5_re'), stack(st_sample, 's5_re'),
            stack(st_prompt, 's5_im'), stack(st_sample, 's5_im'))
```

```python
import functools
import math

import numpy as np
import jax
import jax.numpy as jnp
from jax import lax
from jax.experimental import pallas as pl
from jax.experimental.pallas import tpu as pltpu

F32 = jnp.float32
BF16 = jnp.bfloat16

HEAD_DIM = 64
ROPE_THETA = 10000.0
RMS_EPS = 1e-6
PAGE_SIZE = 128
GLA_HEADS = 4
GLA_DK = 32
GLA_DV = 64
GLA_GATE_RANK = 16
GLA_TAU = 16.0
GLA_CHUNK = 32
NSA_HEADS = 4
CMP_BLOCK = 32
CMP_STRIDE = 16
SEL_BLOCK = 64
SEL_TOPK = 16
WINDOW = 512
FORCE_SCORE = 1e4
S5_GROUP_SIZE = 16
S5_GROUPS = 16
S5_STATE = 64
MOBA_HEADS = 4
MOBA_BLOCK = 256
MOBA_TOPK = 3
GROUP_WIDTH = 256

VMEM_LIMIT_BYTES = 48 * 1024 * 1024


def _cparams(*sem):
    return pltpu.CompilerParams(dimension_semantics=sem, vmem_limit_bytes=VMEM_LIMIT_BYTES)


def _rms(x, g):
    return x * lax.rsqrt(jnp.mean(x * x, axis=-1, keepdims=True) + RMS_EPS) * g


def _pick_tile(m, pref):
    t = min(m, pref)
    while m % t:
        t //= 2
    return t


def _ffn_kernel(x_ref, g_ref, wg_ref, wu_ref, wo_ref, o_ref, xn_ref, acc_ref):
    j = pl.program_id(1)

    @pl.when(j == 0)
    def _():
        xn_ref[...] = _rms(x_ref[...], g_ref[...]).astype(BF16)
        acc_ref[...] = jnp.zeros_like(acc_ref)

    xn = xn_ref[...]
    g = jnp.dot(xn, wg_ref[...], preferred_element_type=F32)
    u = jnp.dot(xn, wu_ref[...], preferred_element_type=F32)
    act = (g * jax.nn.sigmoid(g) * u).astype(BF16)
    acc_ref[...] += jnp.dot(act, wo_ref[...], preferred_element_type=F32)

    @pl.when(j == pl.num_programs(1) - 1)
    def _():
        o_ref[...] = x_ref[...] + 0.5 * acc_ref[...]


def ffn_block(x, gain, w_in, w_out):
    m, d = x.shape
    f = w_out.shape[0]
    tm = _pick_tile(m, 1024)
    tf = 256
    nf = f // tf
    return pl.pallas_call(
        _ffn_kernel,
        grid=(m // tm, nf),
        in_specs=[
            pl.BlockSpec((tm, d), lambda i, j: (i, 0)),
            pl.BlockSpec((1, d), lambda i, j: (0, 0)),
            pl.BlockSpec((d, tf), lambda i, j: (0, j)),
            pl.BlockSpec((d, tf), lambda i, j: (0, j + nf)),
            pl.BlockSpec((tf, d), lambda i, j: (j, 0)),
        ],
        out_specs=pl.BlockSpec((tm, d), lambda i, j: (i, 0)),
        out_shape=jax.ShapeDtypeStruct((m, d), F32),
        scratch_shapes=[pltpu.VMEM((tm, d), BF16), pltpu.VMEM((tm, d), F32)],
        compiler_params=_cparams("parallel", "arbitrary"),
        name="ffn_block",
    )(x, gain.reshape(1, d), w_in, w_in, w_out)


MIXIN_OUT_WIDTHS = (768, 128, 256, 256, 256, 128, 128, 256, 256, 512)


def _mixin_kernel(h_ref, g_ref, w_ref, tab_ref, gw_ref, gb_ref,
                  gla_o, loga_o, nqraw_o, nqrot_o, nsa_o, win_o, gate_o, su_o, mq_o, moba_o):
    xn = _rms(h_ref[...], g_ref[...]).astype(BF16)

    def mm(a, b):
        return jnp.dot(xn, w_ref[:, a:b], preferred_element_type=F32)

    c256, s256 = tab_ref[:, 0:256], tab_ref[:, 256:512]
    gla_o[...] = mm(0, 768)
    z = mm(768, 896)
    la = jnp.dot(z, gw_ref[...], preferred_element_type=F32) + gb_ref[...]
    loga_o[...] = (jnp.minimum(la, 0.0) - jnp.log(1.0 + jnp.exp(-jnp.abs(la)))) * (1.0 / GLA_TAU)
    nq = mm(896, 1152)
    nqraw_o[...] = nq
    nqrot_o[...] = nq * c256 + mm(1152, 1408) * s256
    nsa_o[...] = mm(1408, 1664) * tab_ref[:, 512:768] + mm(1792, 2048) * tab_ref[:, 768:1024]
    win_o[...] = mm(1664, 1792) * tab_ref[:, 1024:1152] + mm(2048, 2176) * tab_ref[:, 1152:1280]
    gate_o[...] = jax.nn.sigmoid(mm(2176, 2304))
    su_o[...] = mm(2304, 2560)
    mq_o[...] = mm(2560, 2816) * c256 + mm(2816, 3072) * s256
    moba_o[:, 0:256] = mm(3072, 3328) * c256 + mm(3328, 3584) * s256
    moba_o[:, 256:512] = mm(3584, 3840)


def _rot_cols(w):
    d, n = w.shape
    w = w.reshape(d, n // HEAD_DIM, 2, HEAD_DIM // 2)
    return jnp.stack([-w[:, :, 1], w[:, :, 0]], axis=2).reshape(d, n)


def _mixin_weights(w):
    d = w.shape[0]
    z = lambda n: jnp.zeros((d, n), w.dtype)
    gq, gk, gv, gz, gog = w[:, 0:128], w[:, 128:256], w[:, 256:512], w[:, 512:528], w[:, 528:784]
    nq, nkv, ng, su = w[:, 784:1040], w[:, 1040:1424], w[:, 1424:1436], w[:, 1436:1692]
    mq, mk, mv = w[:, 1692:1948], w[:, 1948:2204], w[:, 2204:2460]
    r_nsa = jnp.concatenate([z(128), _rot_cols(nkv[:, 128:192]), z(64)], axis=1)
    r_win = jnp.concatenate([_rot_cols(nkv[:, 256:320]), z(64)], axis=1)
    cat = jnp.concatenate([gq, gk, gv, gog, gz, z(112), nq, _rot_cols(nq), nkv, r_nsa, r_win, ng, z(116), su,
                           mq, _rot_cols(mq), mk, _rot_cols(mk), mv], axis=1)
    return cat.astype(BF16)


def _rope_table(pos):
    half = HEAD_DIM // 2
    inv = ROPE_THETA ** (-jnp.arange(half, dtype=F32) / half)
    ang = pos.astype(F32)[:, None] * inv
    c, s = jnp.cos(ang), jnp.sin(ang)
    c64, s64 = jnp.concatenate([c, c], -1), jnp.concatenate([s, s], -1)
    one, zero = jnp.ones_like(c64), jnp.zeros_like(c64)
    return jnp.concatenate([c64, c64, c64, c64, s64, s64, s64, s64, one, one, c64, one, zero, zero, s64, zero,
                            c64, one, s64, zero], axis=-1)


def mixin_block(h, gain, wcat, table, gate_w, gate_b, t_len):
    m, d = h.shape
    tm = _pick_tile(m, 512)
    if t_len % tm == 0:
        ntab = t_len // tm
        tab_map = lambda i: (i % ntab, 0)
    else:
        table = jnp.tile(table, (tm // t_len, 1))
        tab_map = lambda i: (0, 0)
    gw = jnp.zeros((128, 128), F32).at[:GLA_GATE_RANK].set(gate_w)
    return pl.pallas_call(
        _mixin_kernel,
        grid=(m // tm,),
        in_specs=[
            pl.BlockSpec((tm, d), lambda i: (i, 0)),
            pl.BlockSpec((1, d), lambda i: (0, 0)),
            pl.BlockSpec(wcat.shape, lambda i: (0, 0)),
            pl.BlockSpec((tm, table.shape[1]), tab_map),
            pl.BlockSpec((128, 128), lambda i: (0, 0)),
            pl.BlockSpec((1, 128), lambda i: (0, 0)),
        ],
        out_specs=[pl.BlockSpec((tm, n), lambda i: (i, 0)) for n in MIXIN_OUT_WIDTHS],
        out_shape=[jax.ShapeDtypeStruct((m, n), F32) for n in MIXIN_OUT_WIDTHS],
        compiler_params=_cparams("parallel"),
        name="mixin_block",
    )(h, gain.reshape(1, d), wcat, table, gw, gate_b.reshape(1, 128))


def _mixout_kernel(h_ref, a_ref, b_ref, c_ref, d_ref, w_ref, o_ref):
    acc = h_ref[...]
    for k, r in enumerate((a_ref, b_ref, c_ref, d_ref)):
        acc = acc + jnp.dot(r[...].astype(BF16), w_ref[k * GROUP_WIDTH:(k + 1) * GROUP_WIDTH, :],
                            preferred_element_type=F32)
    o_ref[...] = acc


def mixout_block(h, parts, w):
    m, d = h.shape
    tm = _pick_tile(m, 512)
    return pl.pallas_call(
        _mixout_kernel,
        grid=(m // tm,),
        in_specs=[pl.BlockSpec((tm, d), lambda i: (i, 0))]
        + [pl.BlockSpec((tm, GROUP_WIDTH), lambda i: (i, 0)) for _ in range(4)]
        + [pl.BlockSpec((d, d), lambda i: (0, 0))],
        out_specs=pl.BlockSpec((tm, d), lambda i: (i, 0)),
        out_shape=jax.ShapeDtypeStruct((m, d), F32),
        compiler_params=_cparams("parallel"),
        name="mixout_block",
    )(h, *parts, w)


def _ple_kernel(h_ref, p_ref, g_ref, wp_ref, wg_ref, gf_ref, o_ref, *, final):
    h = h_ref[...]
    gate = jax.nn.sigmoid(jnp.dot(_rms(h, g_ref[...]).astype(BF16), wg_ref[...], preferred_element_type=F32))
    e = jnp.dot(p_ref[...].astype(BF16), wp_ref[...], preferred_element_type=F32)
    y = h + e * gate
    o_ref[...] = _rms(y, gf_ref[...]) if final else y


def ple_block(h, p, gain, ple_w, gate_w, final_gain, final):
    m, d = h.shape
    pd = p.shape[1]
    tm = _pick_tile(m, 512)
    return pl.pallas_call(
        functools.partial(_ple_kernel, final=final),
        grid=(m // tm,),
        in_specs=[
            pl.BlockSpec((tm, d), lambda i: (i, 0)),
            pl.BlockSpec((tm, pd), lambda i: (i, 0)),
            pl.BlockSpec((1, d), lambda i: (0, 0)),
            pl.BlockSpec((pd, d), lambda i: (0, 0)),
            pl.BlockSpec((d, d), lambda i: (0, 0)),
            pl.BlockSpec((1, d), lambda i: (0, 0)),
        ],
        out_specs=pl.BlockSpec((tm, d), lambda i: (i, 0)),
        out_shape=jax.ShapeDtypeStruct((m, d), F32),
        compiler_params=_cparams("parallel"),
        name="ple_block",
    )(h, p, gain.reshape(1, d), ple_w, gate_w, final_gain.reshape(1, d))


def _masked_softmax(s, mask):
    s = jnp.where(mask, s, -jnp.inf)
    m = jnp.max(s, axis=-1, keepdims=True)
    m = jnp.where(jnp.isfinite(m), m, 0.0)
    e = jnp.where(mask, jnp.exp(s - m), 0.0)
    return e / jnp.maximum(jnp.sum(e, axis=-1, keepdims=True), 1e-30)


def _topk_mask(score, k):
    n = score.shape[-1]
    a = score[..., :, None]
    b = score[..., None, :]
    idx = jnp.arange(n)
    beats = (b > a) | ((b == a) & (idx[None, :] < idx[:, None]))
    return jnp.sum(beats, axis=-1) < k


def _gla_jnp(gla, loga, s0, chunk, gnorm):
    m = gla.shape[0]
    b = s0.shape[0]
    t = m // b
    q = gla[:, 0:128].reshape(b, t, GLA_HEADS, GLA_DK) * GLA_DK ** -0.5
    k = gla[:, 128:256].reshape(b, t, GLA_HEADS, GLA_DK)
    v = gla[:, 256:512].reshape(b, t, GLA_HEADS, GLA_DV)
    og = gla[:, 512:768].reshape(b, t, GLA_HEADS, GLA_DV)
    log_a = loga.reshape(b, t, GLA_HEADS, GLA_DK)
    tri = jnp.tril(jnp.ones((chunk, chunk), dtype=bool))

    def ch(x):
        return jnp.moveaxis(x.reshape((b, t // chunk, chunk) + x.shape[2:]), 1, 0)

    def step(S, xs):
        qc, kc, vc, gc = xs
        bb = jnp.cumsum(gc, axis=1)
        o_inter = jnp.einsum('bchk,bhkv->bchv', qc * jnp.exp(bb), S)
        diff = bb[:, :, None] - bb[:, None, :]
        decay = jnp.exp(jnp.where(tri[None, :, :, None, None], diff, -jnp.inf))
        att = jnp.einsum('bihk,bjhk,bijhk->bhij', qc, kc, decay)
        o_intra = jnp.einsum('bhij,bjhv->bihv', att, vc)
        b_last = bb[:, -1]
        S = jnp.exp(b_last)[..., None] * S + jnp.einsum('bjhk,bjhv->bhkv', kc * jnp.exp(b_last[:, None] - bb), vc)
        return S, o_inter + o_intra

    s_fin, o = lax.scan(step, s0, (ch(q), ch(k), ch(v), ch(log_a)))
    o = jnp.moveaxis(o, 0, 1).reshape(b, t, GLA_HEADS, GLA_DV)
    o = _rms(o, gnorm) * (og * jax.nn.sigmoid(og))
    return o.reshape(m, GROUP_WIDTH), s_fin


def _compress_jnp(ctx, pe, w1, w2):
    b, L, dh = ctx.shape
    n_cmp = (L - CMP_BLOCK) // CMP_STRIDE + 1
    x = ctx[:, :CMP_STRIDE * (n_cmp + 1)].reshape(b, n_cmp + 1, CMP_STRIDE * dh)
    half = CMP_STRIDE * dh
    a = x @ w1[:half]
    bb = x @ w1[half:]
    hcur = a[:, :n_cmp] + bb[:, 1:] + pe.reshape(1, -1) @ w1
    return jax.nn.gelu(hcur) @ w2


def _nsa_jnp(nq_raw, nq_rot, gates, ctx, win_k, win_v, win_pos, pos, pe, w1, w2):
    b, L, _ = ctx.shape
    t = pos.shape[0]
    scale = HEAD_DIM ** -0.5
    qr = nq_raw.reshape(b, t, NSA_HEADS, HEAD_DIM)
    qs = nq_rot.reshape(b, t, NSA_HEADS, HEAD_DIM)
    kc = _compress_jnp(ctx[:, :, 0:64], pe[0], w1[0], w2[0])
    vc = _compress_jnp(ctx[:, :, 64:128], pe[1], w1[1], w2[1])
    n_cmp = kc.shape[1]
    n_sel = -(-L // SEL_BLOCK)
    cstart = jnp.arange(n_cmp) * CMP_STRIDE
    jstart = jnp.arange(n_sel) * SEL_BLOCK
    overlap = ((cstart[:, None] < jstart[None, :] + SEL_BLOCK) & (cstart[:, None] + CMP_BLOCK > jstart[None, :])).astype(F32)
    cend = cstart + CMP_BLOCK - 1
    sc = jnp.einsum('bqhd,bnd->bqhn', qr, kc) * scale
    pc = _masked_softmax(sc, (cend[None, :] <= pos[:, None])[None, :, None, :])
    o_c = jnp.einsum('bqhn,bnd->bqhd', pc, vc)
    imp = jnp.einsum('bqhn,ns->bqs', pc, overlap)
    cur = pos // SEL_BLOCK
    blk = jnp.arange(n_sel)
    forced = (blk[None, :] == 0) | (blk[None, :] == cur[:, None])
    past = blk[None, :] < cur[:, None]
    imp = jnp.where(forced[None], FORCE_SCORE, jnp.where(past[None], imp, -jnp.inf))
    sel = _topk_mask(imp, min(SEL_TOPK, n_sel)) & jnp.isfinite(imp)
    kpos = jnp.arange(L)
    kmask = sel[:, :, kpos // SEL_BLOCK] & (kpos[None, None, :] <= pos[None, :, None])
    ss = jnp.einsum('bqhd,bkd->bqhk', qs, ctx[:, :, 128:192]) * scale
    ps = _masked_softmax(ss, kmask[:, :, None, :])
    o_s = jnp.einsum('bqhk,bkd->bqhd', ps, ctx[:, :, 192:256])
    diff = pos[:, None] - win_pos[None, :]
    wmask = (diff >= 0) & (diff < WINDOW) & (win_pos[None, :] >= 0)
    sw = jnp.einsum('bqhd,bkd->bqhk', qs, win_k) * scale
    pw = _masked_softmax(sw, wmask[None, :, None, :])
    o_w = jnp.einsum('bqhk,bkd->bqhd', pw, win_v)
    g = gates[:, :12].reshape(b, t, 3, NSA_HEADS, 1)
    o = g[:, :, 0] * o_c + g[:, :, 1] * o_s + g[:, :, 2] * o_w
    return o.reshape(b * t, GROUP_WIDTH)


def _s5_jnp(su, h0, a_re, a_im, b_re, b_im, c_re, c_im, dvec, log_dt, glu_w, glu_b, b):
    m = su.shape[0]
    t = m // b
    uf = su.reshape(b, t, GROUP_WIDTH)
    ug = uf.reshape(b, t, S5_GROUPS, S5_GROUP_SIZE)
    dt = jnp.exp(log_dt)[:, None]
    mag = jnp.exp(a_re * dt)
    abar_re, abar_im = mag * jnp.cos(a_im * dt), mag * jnp.sin(a_im * dt)
    den = a_re * a_re + a_im * a_im
    fr = ((abar_re - 1.0) * a_re + abar_im * a_im) / den
    fi = (abar_im * a_re - (abar_re - 1.0) * a_im) / den
    bbar_re = fr[..., None] * b_re - fi[..., None] * b_im
    bbar_im = fr[..., None] * b_im + fi[..., None] * b_re
    bu_re = jnp.einsum('gnc,btgc->btgn', bbar_re, ug)
    bu_im = jnp.einsum('gnc,btgc->btgn', bbar_im, ug)
    if h0 is None:
        h0 = (jnp.zeros((b, S5_GROUPS, S5_STATE), F32),) * 2

    def step(carry, xs):
        hr, hi = carry
        br, bi = xs
        nr = abar_re * hr - abar_im * hi + br
        ni = abar_re * hi + abar_im * hr + bi
        return (nr, ni), (nr, ni)

    (fr_, fi_), (h_re, h_im) = lax.scan(step, h0, (jnp.moveaxis(bu_re, 1, 0), jnp.moveaxis(bu_im, 1, 0)))
    h_re, h_im = jnp.moveaxis(h_re, 0, 1), jnp.moveaxis(h_im, 0, 1)
    y = jnp.einsum('gcn,btgn->btgc', c_re, h_re) - jnp.einsum('gcn,btgn->btgc', c_im, h_im)
    y = y.reshape(b, t, GROUP_WIDTH) + dvec * uf
    g = jax.nn.gelu(y)
    out = g * jax.nn.sigmoid(g @ glu_w + glu_b)
    return out.reshape(m, GROUP_WIDTH), fr_, fi_


def _moba_jnp(mq, k_ctx, v_ctx, pos):
    b, L = k_ctx.shape[:2]
    t = pos.shape[0]
    q = mq.reshape(b, t, MOBA_HEADS, HEAD_DIM)
    n_full = L // MOBA_BLOCK
    n_cols = max(n_full, MOBA_TOPK)
    means = k_ctx[:, :n_full * MOBA_BLOCK].reshape(b, n_full, MOBA_BLOCK, MOBA_HEADS, HEAD_DIM).mean(axis=2)
    means = jnp.pad(means, ((0, 0), (0, n_cols - n_full), (0, 0), (0, 0)))
    own = pos // MOBA_BLOCK
    s = jnp.einsum('bqhd,bnhd->bqhn', q, means)
    cols = jnp.arange(n_cols)
    s = jnp.where((cols[None, :] < own[:, None])[None, :, None, :], s, -jnp.inf)
    sel = _topk_mask(s, MOBA_TOPK) & jnp.isfinite(s)
    kpos = jnp.arange(L)
    kb = kpos // MOBA_BLOCK
    selk = jnp.where(kb[None, None, None, :] < n_cols, sel[..., jnp.minimum(kb, n_cols - 1)], False)
    mask = (selk | (kb[None, :] == own[:, None])[None, :, None, :]) & (kpos[None, :] <= pos[:, None])[None, :, None, :]
    sc = jnp.einsum('bqhd,bkhd->bqhk', q, k_ctx) * HEAD_DIM ** -0.5
    pr = _masked_softmax(sc, mask)
    return jnp.einsum('bqhk,bkhd->bqhd', pr, v_ctx).reshape(b * t, GROUP_WIDTH)


def _layer(h, p_emb, b, t, pos0, lw, past, final_gain, final):
    prompt = past is None
    pos = pos0 + jnp.arange(t, dtype=jnp.int32)
    h = ffn_block(h, lw['norm_ffn1'], lw['ffn1_w_in'], lw['ffn1_w_out'])
    (gla, loga, nq_raw, nq_rot, nsa_rows, win_rows, gates, su, mq, moba_rows) = mixin_block(
        h, lw['norm_mix'], lw['wcat'], _rope_table(pos), lw['gla_gate_w'], lw['gla_gate_b'], t)

    s0 = jnp.zeros((b, GLA_HEADS, GLA_DK, GLA_DV), F32) if prompt else past['gla']
    o_gla, gla_state = _gla_jnp(gla, loga, s0, GLA_CHUNK if prompt else t, lw['gla_norm'])

    rows3 = nsa_rows.reshape(b, t, 4 * HEAD_DIM)
    win3 = win_rows.reshape(b, t, 2 * HEAD_DIM)
    if prompt:
        ctx = rows3
        win_k, win_v, win_pos = win3[:, :, :64], win3[:, :, 64:], pos
        new_win = win3[:, -min(WINDOW, t):]
    else:
        ctx = jnp.concatenate([past['nsa'], rows3], axis=1)
        wb = past['win'].reshape(b, -1, 2 * HEAD_DIM)
        wl = wb.shape[1]
        wcat = jnp.concatenate([wb, win3], axis=1)
        win_k, win_v = wcat[:, :, :64], wcat[:, :, 64:]
        win_pos = pos0 - wl + jnp.arange(wl + t, dtype=jnp.int32)
        new_win = wcat[:, -wl:]
    o_nsa = _nsa_jnp(nq_raw, nq_rot, gates, ctx, win_k, win_v, win_pos, pos,
                     lw['nsa_cmp_pe'], lw['nsa_cmp_w1'], lw['nsa_cmp_w2'])

    h0 = None if prompt else (past['s5_re'], past['s5_im'])
    o_s5, s5_re, s5_im = _s5_jnp(su, h0, lw['s5_a_re'], lw['s5_a_im'], lw['s5_b_re'], lw['s5_b_im'], lw['s5_c_re'],
                                 lw['s5_c_im'], lw['s5_d'], lw['s5_log_dt'], lw['s5_glu_w'], lw['s5_glu_b'], b)

    mrows = moba_rows.reshape(b, t, 2, MOBA_HEADS, HEAD_DIM)
    mctx = mrows if prompt else jnp.concatenate([past['moba'], mrows], axis=1)
    o_moba = _moba_jnp(mq, mctx[:, :, 0], mctx[:, :, 1], pos)

    h = mixout_block(h, (o_gla, o_nsa, o_s5, o_moba), lw['w_mix_out'])
    h = ffn_block(h, lw['norm_ffn2'], lw['ffn2_w_in'], lw['ffn2_w_out'])
    h = ple_block(h, p_emb, lw['norm_ple'], lw['ple_w'], lw['ple_gate_w'], final_gain, final)
    state = {'nsa': nsa_rows.reshape(b, t, 4, HEAD_DIM), 'moba': mrows,
             'win': new_win.reshape(b, -1, 2, HEAD_DIM), 'gla': gla_state, 's5_re': s5_re, 's5_im': s5_im}
    return h, state


def kernel(x_prompt, x_sample, p_prompt, p_sample, cache_nsa_kv, cache_moba_kv, state_nsa_win, state_gla,
           state_s5_re, state_s5_im, page_table, norm_ffn1, ffn1_w_in, ffn1_w_out, norm_mix, w_mix_in, w_mix_out,
           gla_gate_w, gla_gate_b, gla_norm, nsa_cmp_pe, nsa_cmp_w1, nsa_cmp_w2, s5_a_re, s5_a_im, s5_b_re, s5_b_im,
           s5_c_re, s5_c_im, s5_d, s5_log_dt, s5_glu_w, s5_glu_b, norm_ffn2, ffn2_w_in, ffn2_w_out, norm_ple, ple_w,
           ple_gate_w, norm_final):
    bp, tp, d = x_prompt.shape
    bs, ts, _ = x_sample.shape
    depth = p_prompt.shape[0]
    past_len = page_table.shape[1] * PAGE_SIZE
    params = {'norm_ffn1': norm_ffn1, 'norm_mix': norm_mix, 'gla_gate_w': gla_gate_w, 'gla_gate_b': gla_gate_b,
              'gla_norm': gla_norm, 'nsa_cmp_pe': nsa_cmp_pe, 'nsa_cmp_w1': nsa_cmp_w1, 'nsa_cmp_w2': nsa_cmp_w2,
              's5_a_re': s5_a_re, 's5_a_im': s5_a_im, 's5_b_re': s5_b_re, 's5_b_im': s5_b_im, 's5_c_re': s5_c_re,
              's5_c_im': s5_c_im, 's5_d': s5_d, 's5_log_dt': s5_log_dt, 's5_glu_w': s5_glu_w, 's5_glu_b': s5_glu_b,
              'norm_ffn2': norm_ffn2, 'norm_ple': norm_ple}
    hp = x_prompt.reshape(bp * tp, d)
    hs = x_sample.reshape(bs * ts, d)
    st_p, st_s = [], []
    for i in range(depth):
        lw = {k: v[i] for k, v in params.items()}
        lw['wcat'] = _mixin_weights(w_mix_in[i])
        for name, w in (('ffn1_w_in', ffn1_w_in), ('ffn1_w_out', ffn1_w_out), ('ffn2_w_in', ffn2_w_in),
                        ('ffn2_w_out', ffn2_w_out), ('w_mix_out', w_mix_out), ('ple_w', ple_w),
                        ('ple_gate_w', ple_gate_w)):
            lw[name] = w[i].astype(BF16)
        final = i == depth - 1
        hp, sp = _layer(hp, p_prompt[i].reshape(bp * tp, -1), bp, tp, 0, lw, None, norm_final, final)
        nsa_past = lax.optimization_barrier(cache_nsa_kv[i, page_table]).reshape(bs, past_len, 4 * HEAD_DIM)
        moba_past = lax.optimization_barrier(cache_moba_kv[i, page_table]).reshape(bs, past_len, 2, MOBA_HEADS, HEAD_DIM)
        past = {'nsa': nsa_past, 'moba': moba_past, 'win': state_nsa_win[i], 'gla': state_gla[i],
                's5_re': state_s5_re[i], 's5_im': state_s5_im[i]}
        hs, ss = _layer(hs, p_sample[i].reshape(bs * ts, -1), bs, ts, past_len, lw, past, norm_final, final)
        st_p.append(sp)
        st_s.append(ss)

    def stack(states, name):
        return jnp.stack([s[name] for s in states], axis=0)

    return (hp.reshape(bp, tp, d), hs.reshape(bs, ts, d),
            stack(st_p, 'nsa'), stack(st_s, 'nsa'), stack(st_p, 'moba'), stack(st_s, 'moba'),
            stack(st_p, 'win'), stack(st_s, 'win'), stack(st_p, 'gla'), stack(st_s, 'gla'),
            stack(st_p, 's5_re'), stack(st_s, 's5_re'), stack(st_p, 's5_im'), stack(st_s, 's5_im'))
```

```python
import functools
import math

import numpy as np
import jax
import jax.numpy as jnp
from jax import lax
from jax.experimental import pallas as pl
from jax.experimental.pallas import tpu as pltpu

F32 = jnp.float32
BF16 = jnp.bfloat16

HEAD_DIM = 64
ROPE_THETA = 10000.0
RMS_EPS = 1e-6
PAGE_SIZE = 128
GLA_HEADS = 4
GLA_DK = 32
GLA_DV = 64
GLA_GATE_RANK = 16
GLA_TAU = 16.0
GLA_CHUNK = 32
NSA_HEADS = 4
CMP_BLOCK = 32
CMP_STRIDE = 16
SEL_BLOCK = 64
SEL_TOPK = 16
WINDOW = 512
FORCE_SCORE = 1e4
S5_GROUP_SIZE = 16
S5_GROUPS = 16
S5_STATE = 64
MOBA_HEADS = 4
MOBA_BLOCK = 256
MOBA_TOPK = 3
GROUP_WIDTH = 256

VMEM_LIMIT_BYTES = 48 * 1024 * 1024


def _cparams(*sem):
    return pltpu.CompilerParams(dimension_semantics=sem, vmem_limit_bytes=VMEM_LIMIT_BYTES)


def _rms(x, g):
    return x * lax.rsqrt(jnp.mean(x * x, axis=-1, keepdims=True) + RMS_EPS) * g


def _pick_tile(m, pref):
    t = min(m, pref)
    while m % t:
        t //= 2
    return t


def _ffn_kernel(x_ref, g_ref, wg_ref, wu_ref, wo_ref, o_ref, xn_ref, acc_ref):
    j = pl.program_id(1)

    @pl.when(j == 0)
    def _():
        xn_ref[...] = _rms(x_ref[...], g_ref[...]).astype(BF16)
        acc_ref[...] = jnp.zeros_like(acc_ref)

    xn = xn_ref[...]
    g = jnp.dot(xn, wg_ref[...], preferred_element_type=F32)
    u = jnp.dot(xn, wu_ref[...], preferred_element_type=F32)
    act = (g * jax.nn.sigmoid(g) * u).astype(BF16)
    acc_ref[...] += jnp.dot(act, wo_ref[...], preferred_element_type=F32)

    @pl.when(j == pl.num_programs(1) - 1)
    def _():
        o_ref[...] = x_ref[...] + 0.5 * acc_ref[...]


def ffn_block(x, gain, w_in, w_out):
    m, d = x.shape
    f = w_out.shape[0]
    tm = _pick_tile(m, 1024)
    tf = 256
    nf = f // tf
    return pl.pallas_call(
        _ffn_kernel,
        grid=(m // tm, nf),
        in_specs=[
            pl.BlockSpec((tm, d), lambda i, j: (i, 0)),
            pl.BlockSpec((1, d), lambda i, j: (0, 0)),
            pl.BlockSpec((d, tf), lambda i, j: (0, j)),
            pl.BlockSpec((d, tf), lambda i, j: (0, j + nf)),
            pl.BlockSpec((tf, d), lambda i, j: (j, 0)),
        ],
        out_specs=pl.BlockSpec((tm, d), lambda i, j: (i, 0)),
        out_shape=jax.ShapeDtypeStruct((m, d), F32),
        scratch_shapes=[pltpu.VMEM((tm, d), BF16), pltpu.VMEM((tm, d), F32)],
        compiler_params=_cparams("parallel", "arbitrary"),
        name="ffn_block",
    )(x, gain.reshape(1, d), w_in, w_in, w_out)


MIXIN_OUT_WIDTHS = (768, 128, 256, 256, 256, 128, 128, 256, 256, 512)


def _mixin_kernel(h_ref, g_ref, w_ref, tab_ref, gw_ref, gb_ref,
                  gla_o, loga_o, nqraw_o, nqrot_o, nsa_o, win_o, gate_o, su_o, mq_o, moba_o):
    xn = _rms(h_ref[...], g_ref[...]).astype(BF16)

    def mm(a, b):
        return jnp.dot(xn, w_ref[:, a:b], preferred_element_type=F32)

    c256, s256 = tab_ref[:, 0:256], tab_ref[:, 256:512]
    gla_o[...] = mm(0, 768)
    z = mm(768, 896)
    la = jnp.dot(z, gw_ref[...], preferred_element_type=F32) + gb_ref[...]
    loga_o[...] = (jnp.minimum(la, 0.0) - jnp.log(1.0 + jnp.exp(-jnp.abs(la)))) * (1.0 / GLA_TAU)
    nq = mm(896, 1152)
    nqraw_o[...] = nq
    nqrot_o[...] = nq * c256 + mm(1152, 1408) * s256
    nsa_o[...] = mm(1408, 1664) * tab_ref[:, 512:768] + mm(1792, 2048) * tab_ref[:, 768:1024]
    win_o[...] = mm(1664, 1792) * tab_ref[:, 1024:1152] + mm(2048, 2176) * tab_ref[:, 1152:1280]
    gate_o[...] = jax.nn.sigmoid(mm(2176, 2304))
    su_o[...] = mm(2304, 2560)
    mq_o[...] = mm(2560, 2816) * c256 + mm(2816, 3072) * s256
    moba_o[:, 0:256] = mm(3072, 3328) * c256 + mm(3328, 3584) * s256
    moba_o[:, 256:512] = mm(3584, 3840)


def _rot_cols(w):
    d, n = w.shape
    w = w.reshape(d, n // HEAD_DIM, 2, HEAD_DIM // 2)
    return jnp.stack([-w[:, :, 1], w[:, :, 0]], axis=2).reshape(d, n)


def _mixin_weights(w):
    d = w.shape[0]
    z = lambda n: jnp.zeros((d, n), w.dtype)
    gq, gk, gv, gz, gog = w[:, 0:128], w[:, 128:256], w[:, 256:512], w[:, 512:528], w[:, 528:784]
    nq, nkv, ng, su = w[:, 784:1040], w[:, 1040:1424], w[:, 1424:1436], w[:, 1436:1692]
    mq, mk, mv = w[:, 1692:1948], w[:, 1948:2204], w[:, 2204:2460]
    r_nsa = jnp.concatenate([z(128), _rot_cols(nkv[:, 128:192]), z(64)], axis=1)
    r_win = jnp.concatenate([_rot_cols(nkv[:, 256:320]), z(64)], axis=1)
    cat = jnp.concatenate([gq, gk, gv, gog, gz, z(112), nq, _rot_cols(nq), nkv, r_nsa, r_win, ng, z(116), su,
                           mq, _rot_cols(mq), mk, _rot_cols(mk), mv], axis=1)
    return cat.astype(BF16)


def _rope_table(pos):
    half = HEAD_DIM // 2
    inv = ROPE_THETA ** (-jnp.arange(half, dtype=F32) / half)
    ang = pos.astype(F32)[:, None] * inv
    c, s = jnp.cos(ang), jnp.sin(ang)
    c64, s64 = jnp.concatenate([c, c], -1), jnp.concatenate([s, s], -1)
    one, zero = jnp.ones_like(c64), jnp.zeros_like(c64)
    return jnp.concatenate([c64, c64, c64, c64, s64, s64, s64, s64, one, one, c64, one, zero, zero, s64, zero,
                            c64, one, s64, zero], axis=-1)


def mixin_block(h, gain, wcat, table, gate_w, gate_b, t_len):
    m, d = h.shape
    tm = _pick_tile(m, 512)
    if t_len % tm == 0:
        ntab = t_len // tm
        tab_map = lambda i: (i % ntab, 0)
    else:
        table = jnp.tile(table, (tm // t_len, 1))
        tab_map = lambda i: (0, 0)
    gw = jnp.zeros((128, 128), F32).at[:GLA_GATE_RANK].set(gate_w)
    return pl.pallas_call(
        _mixin_kernel,
        grid=(m // tm,),
        in_specs=[
            pl.BlockSpec((tm, d), lambda i: (i, 0)),
            pl.BlockSpec((1, d), lambda i: (0, 0)),
            pl.BlockSpec(wcat.shape, lambda i: (0, 0)),
            pl.BlockSpec((tm, table.shape[1]), tab_map),
            pl.BlockSpec((128, 128), lambda i: (0, 0)),
            pl.BlockSpec((1, 128), lambda i: (0, 0)),
        ],
        out_specs=[pl.BlockSpec((tm, n), lambda i: (i, 0)) for n in MIXIN_OUT_WIDTHS],
        out_shape=[jax.ShapeDtypeStruct((m, n), F32) for n in MIXIN_OUT_WIDTHS],
        compiler_params=_cparams("parallel"),
        name="mixin_block",
    )(h, gain.reshape(1, d), wcat, table, gw, gate_b.reshape(1, 128))


def _mixout_kernel(h_ref, a_ref, b_ref, c_ref, d_ref, w_ref, o_ref):
    acc = h_ref[...]
    for k, r in enumerate((a_ref, b_ref, c_ref, d_ref)):
        acc = acc + jnp.dot(r[...].astype(BF16), w_ref[k * GROUP_WIDTH:(k + 1) * GROUP_WIDTH, :],
                            preferred_element_type=F32)
    o_ref[...] = acc


def mixout_block(h, parts, w):
    m, d = h.shape
    tm = _pick_tile(m, 512)
    return pl.pallas_call(
        _mixout_kernel,
        grid=(m // tm,),
        in_specs=[pl.BlockSpec((tm, d), lambda i: (i, 0))]
        + [pl.BlockSpec((tm, GROUP_WIDTH), lambda i: (i, 0)) for _ in range(4)]
        + [pl.BlockSpec((d, d), lambda i: (0, 0))],
        out_specs=pl.BlockSpec((tm, d), lambda i: (i, 0)),
        out_shape=jax.ShapeDtypeStruct((m, d), F32),
        compiler_params=_cparams("parallel"),
        name="mixout_block",
    )(h, *parts, w)


def _ple_kernel(h_ref, p_ref, g_ref, wp_ref, wg_ref, gf_ref, o_ref, *, final):
    h = h_ref[...]
    gate = jax.nn.sigmoid(jnp.dot(_rms(h, g_ref[...]).astype(BF16), wg_ref[...], preferred_element_type=F32))
    e = jnp.dot(p_ref[...].astype(BF16), wp_ref[...], preferred_element_type=F32)
    y = h + e * gate
    o_ref[...] = _rms(y, gf_ref[...]) if final else y


def ple_block(h, p, gain, ple_w, gate_w, final_gain, final):
    m, d = h.shape
    pd = p.shape[1]
    tm = _pick_tile(m, 512)
    return pl.pallas_call(
        functools.partial(_ple_kernel, final=final),
        grid=(m // tm,),
        in_specs=[
            pl.BlockSpec((tm, d), lambda i: (i, 0)),
            pl.BlockSpec((tm, pd), lambda i: (i, 0)),
            pl.BlockSpec((1, d), lambda i: (0, 0)),
            pl.BlockSpec((pd, d), lambda i: (0, 0)),
            pl.BlockSpec((d, d), lambda i: (0, 0)),
            pl.BlockSpec((1, d), lambda i: (0, 0)),
        ],
        out_specs=pl.BlockSpec((tm, d), lambda i: (i, 0)),
        out_shape=jax.ShapeDtypeStruct((m, d), F32),
        compiler_params=_cparams("parallel"),
        name="ple_block",
    )(h, p, gain.reshape(1, d), ple_w, gate_w, final_gain.reshape(1, d))


S5_LANES = S5_GROUPS * S5_STATE
S5_TILE = 8
HIGHEST = lax.Precision.HIGHEST


def _s5_kernel(u_ref, h0_ref, pw_ref, bcat_ref, ccat_ref, d_ref, gw_ref, gb_ref,
               o_ref, hre_ref, him_ref, bu_ref, h_ref, carry_ref):
    n = S5_LANES
    tb = u_ref.shape[0]

    @pl.when(pl.program_id(1) == 0)
    def _():
        carry_ref[...] = jnp.broadcast_to(h0_ref[0], (S5_TILE, 2 * n))

    u = u_ref[...]
    bu_ref[...] = jnp.dot(u.astype(BF16), bcat_ref[...], preferred_element_type=F32)

    def tile(k, carry):
        cr, ci = carry
        r0 = pl.multiple_of(k * S5_TILE, S5_TILE)
        xr = bu_ref[pl.ds(r0, S5_TILE), 0:n]
        xi = bu_ref[pl.ds(r0, S5_TILE), n:2 * n]
        for lvl, sh in enumerate((1, 2, 4)):
            pr = pw_ref[:, (2 * lvl) * n:(2 * lvl + 1) * n]
            pi = pw_ref[:, (2 * lvl + 1) * n:(2 * lvl + 2) * n]
            rr = pltpu.roll(xr, sh, 0)
            ri = pltpu.roll(xi, sh, 0)
            xr, xi = xr + pr * rr - pi * ri, xi + pr * ri + pi * rr
        pr, pi = pw_ref[:, 6 * n:7 * n], pw_ref[:, 7 * n:8 * n]
        xr, xi = xr + pr * cr - pi * ci, xi + pr * ci + pi * cr
        h_ref[pl.ds(r0, S5_TILE), 0:n] = xr
        h_ref[pl.ds(r0, S5_TILE), n:2 * n] = xi
        last_r = jnp.broadcast_to(xr[S5_TILE - 1:S5_TILE, :], (S5_TILE, n))
        last_i = jnp.broadcast_to(xi[S5_TILE - 1:S5_TILE, :], (S5_TILE, n))
        return last_r, last_i

    cr, ci = lax.fori_loop(0, tb // S5_TILE, tile, (carry_ref[:, 0:n], carry_ref[:, n:2 * n]))
    carry_ref[:, 0:n] = cr
    carry_ref[:, n:2 * n] = ci
    hre_ref[0] = cr[0:1, :]
    him_ref[0] = ci[0:1, :]

    y = jnp.dot(h_ref[...].astype(BF16), ccat_ref[...], preferred_element_type=F32) + d_ref[...] * u
    g = jax.nn.gelu(y)
    gate = jnp.dot(g.astype(BF16), gw_ref[...], preferred_element_type=F32) + gb_ref[...]
    o_ref[...] = g * jax.nn.sigmoid(gate)


def _s5_params(a_re, a_im, b_re, b_im, c_re, c_im, log_dt):
    g, n = a_re.shape
    c = b_re.shape[-1]
    dt = jnp.exp(log_dt)[:, None]
    mag = jnp.exp(a_re * dt)
    abar_re, abar_im = mag * jnp.cos(a_im * dt), mag * jnp.sin(a_im * dt)
    den = a_re * a_re + a_im * a_im
    fr = ((abar_re - 1.0) * a_re + abar_im * a_im) / den
    fi = (abar_im * a_re - (abar_re - 1.0) * a_im) / den
    bbar_re = fr[..., None] * b_re - fi[..., None] * b_im
    bbar_im = fr[..., None] * b_im + fi[..., None] * b_re
    eye = jnp.eye(g, dtype=F32)
    blk_in = lambda m: jnp.einsum('gnc,gh->gchn', m, eye).reshape(g * c, g * n)
    blk_out = lambda m: jnp.einsum('gcn,gh->gnhc', m, eye).reshape(g * n, g * c)
    bcat = jnp.concatenate([blk_in(bbar_re), blk_in(bbar_im)], axis=1).astype(BF16)
    ccat = jnp.concatenate([blk_out(c_re), -blk_out(c_im)], axis=0).astype(BF16)
    ar, ai = abar_re.reshape(1, g * n), abar_im.reshape(1, g * n)
    pows = [(ar, ai)]
    for _ in range(S5_TILE - 1):
        pr, pi = pows[-1]
        pows.append((pr * ar - pi * ai, pr * ai + pi * ar))
    t = jnp.arange(S5_TILE)[:, None]
    cols = []
    for sh in (1, 2, 4):
        pr, pi = pows[sh - 1]
        cols += [jnp.where(t >= sh, pr, 0.0), jnp.where(t >= sh, pi, 0.0)]
    cols += [jnp.concatenate([p[0] for p in pows], axis=0), jnp.concatenate([p[1] for p in pows], axis=0)]
    return jnp.concatenate(cols, axis=1), bcat, ccat


def s5_block(su, h0, pw, bcat, ccat, dvec, glu_w, glu_b, b):
    m = su.shape[0]
    t = m // b
    tb = _pick_tile(t, 256)
    nt = t // tb
    n = S5_LANES
    out, hre, him = pl.pallas_call(
        _s5_kernel,
        grid=(b, nt),
        in_specs=[
            pl.BlockSpec((tb, GROUP_WIDTH), lambda i, j: (i * nt + j, 0)),
            pl.BlockSpec((1, 1, 2 * n), lambda i, j: (i, 0, 0)),
            pl.BlockSpec(pw.shape, lambda i, j: (0, 0)),
            pl.BlockSpec(bcat.shape, lambda i, j: (0, 0)),
            pl.BlockSpec(ccat.shape, lambda i, j: (0, 0)),
            pl.BlockSpec((1, GROUP_WIDTH), lambda i, j: (0, 0)),
            pl.BlockSpec((GROUP_WIDTH, GROUP_WIDTH), lambda i, j: (0, 0)),
            pl.BlockSpec((1, GROUP_WIDTH), lambda i, j: (0, 0)),
        ],
        out_specs=[
            pl.BlockSpec((tb, GROUP_WIDTH), lambda i, j: (i * nt + j, 0)),
            pl.BlockSpec((1, 1, n), lambda i, j: (i, 0, 0)),
            pl.BlockSpec((1, 1, n), lambda i, j: (i, 0, 0)),
        ],
        out_shape=[jax.ShapeDtypeStruct((m, GROUP_WIDTH), F32),
                   jax.ShapeDtypeStruct((b, 1, n), F32), jax.ShapeDtypeStruct((b, 1, n), F32)],
        scratch_shapes=[pltpu.VMEM((tb, 2 * n), F32), pltpu.VMEM((tb, 2 * n), F32), pltpu.VMEM((S5_TILE, 2 * n), F32)],
        compiler_params=_cparams("parallel", "arbitrary"),
        name="s5_block",
    )(su, h0, pw, bcat, ccat, dvec.reshape(1, -1), glu_w, glu_b.reshape(1, -1))
    return out, hre.reshape(b, S5_GROUPS, S5_STATE), him.reshape(b, S5_GROUPS, S5_STATE)


def _gla_consts(c):
    dk = GLA_DK
    tril = np.tril(np.ones((c, c), np.float32))
    lane_j = np.arange(c * dk) // dk
    diag = (np.arange(c)[:, None] == lane_j[None, :]).astype(np.float32)
    causal = (lane_j[None, :] <= np.arange(c)[:, None]).astype(np.float32)
    gsum = (lane_j[:, None] == np.arange(c)[None, :]).astype(np.float32)
    return jnp.asarray(tril), jnp.asarray(diag), jnp.asarray(causal), jnp.asarray(gsum, dtype=BF16)


def _gla_kernel(x_ref, la_ref, s0_ref, tril_ref, diag_ref, causal_ref, gsum_ref, gn_ref,
                o_ref, sfin_ref, s_ref, *, chunk):
    c, dk, dv = chunk, GLA_DK, GLA_DV
    tg = x_ref.shape[0]

    @pl.when(pl.program_id(1) == 0)
    def _():
        s_ref[...] = s0_ref[0]

    lane_grp = lax.broadcasted_iota(jnp.int32, (c, 128), 1) // dk

    def tile_head(x, h):
        v = x
        for s in range(1, GLA_HEADS):
            v = jnp.where(lane_grp == (h + s) % GLA_HEADS, pltpu.roll(x, s * dk, 1), v)
        return jnp.concatenate([v] * (c * dk // 128), axis=1)

    def body(ci, _):
        r0 = pl.multiple_of(ci * c, c)
        q = x_ref[pl.ds(r0, c), 0:128] * dk ** -0.5
        k = x_ref[pl.ds(r0, c), 128:256]
        la = la_ref[pl.ds(r0, c), :]
        b = jnp.dot(tril_ref[...], la, precision=HIGHEST, preferred_element_type=F32)
        bt = b.T
        kt = k.T
        b_last_t = bt[:, c - 1:c]
        kdt = kt * jnp.exp(b_last_t - bt)
        qd = q * jnp.exp(b)
        diag, causal = diag_ref[...], causal_ref[...]
        for h in range(GLA_HEADS):
            v = x_ref[pl.ds(r0, c), 256 + h * dv:256 + (h + 1) * dv]
            og = x_ref[pl.ds(r0, c), 512 + h * dv:512 + (h + 1) * dv]
            s_h = s_ref[h * dk:(h + 1) * dk, :]
            b_i, q_i, k_i = tile_head(b, h), tile_head(q, h), tile_head(k, h)
            b_j = jnp.sum(b_i * diag, axis=0, keepdims=True)
            k_j = jnp.sum(k_i * diag, axis=0, keepdims=True)
            p = q_i * k_j * jnp.exp(jnp.minimum(b_i - b_j, 0.0)) * causal
            att = jnp.dot(p.astype(BF16), gsum_ref[...], preferred_element_type=F32)
            o = jnp.dot(att.astype(BF16), v.astype(BF16), preferred_element_type=F32)
            o = o + jnp.dot(qd[:, h * dk:(h + 1) * dk].astype(BF16), s_h.astype(BF16), preferred_element_type=F32)
            s_ref[h * dk:(h + 1) * dk, :] = jnp.exp(b_last_t[h * dk:(h + 1) * dk, :]) * s_h + jnp.dot(
                kdt[h * dk:(h + 1) * dk, :].astype(BF16), v.astype(BF16), preferred_element_type=F32)
            o = o * lax.rsqrt(jnp.mean(o * o, axis=-1, keepdims=True) + RMS_EPS) * gn_ref[...]
            o_ref[pl.ds(r0, c), h * dv:(h + 1) * dv] = o * (og * jax.nn.sigmoid(og))
        return 0

    lax.fori_loop(0, tg // c, body, 0)

    @pl.when(pl.program_id(1) == pl.num_programs(1) - 1)
    def _():
        sfin_ref[0] = s_ref[...]


def gla_block(gla, loga, s0, gnorm, b, chunk):
    m = gla.shape[0]
    t = m // b
    tg = _pick_tile(t, 256)
    nt = t // tg
    consts = _gla_consts(chunk)
    hk = GLA_HEADS * GLA_DK
    out, sfin = pl.pallas_call(
        functools.partial(_gla_kernel, chunk=chunk),
        grid=(b, nt),
        in_specs=[
            pl.BlockSpec((tg, 768), lambda i, j: (i * nt + j, 0)),
            pl.BlockSpec((tg, 128), lambda i, j: (i * nt + j, 0)),
            pl.BlockSpec((1, hk, GLA_DV), lambda i, j: (i, 0, 0)),
        ] + [pl.BlockSpec(a.shape, lambda i, j: (0, 0)) for a in consts]
        + [pl.BlockSpec((1, GLA_DV), lambda i, j: (0, 0))],
        out_specs=[
            pl.BlockSpec((tg, GROUP_WIDTH), lambda i, j: (i * nt + j, 0)),
            pl.BlockSpec((1, hk, GLA_DV), lambda i, j: (i, 0, 0)),
        ],
        out_shape=[jax.ShapeDtypeStruct((m, GROUP_WIDTH), F32), jax.ShapeDtypeStruct((b, hk, GLA_DV), F32)],
        scratch_shapes=[pltpu.VMEM((hk, GLA_DV), F32)],
        compiler_params=_cparams("parallel", "arbitrary"),
        name="gla_block",
    )(gla, loga, s0, *consts, gnorm.reshape(1, GLA_DV))
    return out, sfin


KEY_CHUNK = 2048
NEG_INF = float("-inf")
NT_DIMS = (((1,), (1,)), ((), ()))


def _dot_nt(a, b):
    return lax.dot_general(a, b, NT_DIMS, preferred_element_type=F32)


def _dot3(a, b):
    hi = a.astype(BF16)
    r1 = a - hi.astype(F32)
    mid = r1.astype(BF16)
    lo = (r1 - mid.astype(F32)).astype(BF16)
    return (jnp.dot(hi, b, preferred_element_type=F32) + jnp.dot(mid, b, preferred_element_type=F32)
            + jnp.dot(lo, b, preferred_element_type=F32))


def _softmax_parts(scores, masks):
    neg = [jnp.where(mk, s, NEG_INF) for s, mk in zip(scores, masks)]
    m = functools.reduce(jnp.maximum, [jnp.max(s, axis=-1, keepdims=True) for s in neg])
    m = jnp.where(m == NEG_INF, 0.0, m)
    es = [jnp.where(mk, jnp.exp(s - m), 0.0) for s, mk in zip(scores, masks)]
    tot = functools.reduce(jnp.add, [jnp.sum(e, axis=-1, keepdims=True) for e in es])
    return es, 1.0 / jnp.maximum(tot, 1e-30)


def _topk_select(score, jidx, k):
    big = jnp.float32(1e9)
    work, sel = score, jnp.zeros(score.shape, F32)
    for _ in range(k):
        m = jnp.max(work, axis=-1, keepdims=True)
        idx = jnp.min(jnp.where(work == m, jidx, big), axis=-1, keepdims=True)
        pick = jidx == idx
        sel = jnp.where(pick, 1.0, sel)
        work = jnp.where(pick, NEG_INF, work)
    return jnp.where(score > NEG_INF, sel, 0.0)


def _stack_heads(x, pad_to=None):
    parts = [x[:, h * HEAD_DIM:(h + 1) * HEAD_DIM] for h in range(x.shape[1] // HEAD_DIM)]
    out = jnp.concatenate(parts, axis=0)
    if pad_to is not None:
        out = jnp.concatenate([out, jnp.zeros((out.shape[0], pad_to - HEAD_DIM), out.dtype)], axis=1)
    return out


def _block_expand_matrix(block):
    return jnp.asarray(np.arange(KEY_CHUNK)[None, :] // block == np.arange(128)[:, None], dtype=BF16)


def _page_specs(n_pages, width, layer):
    def idx(b, j, pt, *, k):
        return (layer, pt[b, j * n_pages + k], 0, 0)
    return [pl.BlockSpec((None, None, PAGE_SIZE, width), functools.partial(idx, k=k)) for k in range(n_pages)]


def _nsa_prep_kernel(*refs, n_src, paged):
    if paged:
        refs = refs[1:]
    srcs, wc_ref, slc_ref, p_ref, xs_ref = refs[:n_src], refs[n_src], refs[n_src + 1], refs[n_src + 2], refs[n_src + 3]
    j = pl.program_id(1)
    rows = srcs[0].shape[0]
    for k, src in enumerate(srcs):
        r0 = pl.multiple_of((j * n_src + k) * rows, rows)
        xs_ref[pl.ds(r0, rows), :] = src[:, 0:128]
        slc_ref[0, k * rows:(k + 1) * rows, :] = src[:, 128:256].astype(BF16)

    @pl.when(j == pl.num_programs(1) - 1)
    def _():
        n = xs_ref.shape[0] // CMP_STRIDE
        acc = jnp.zeros((n, 512), F32)
        for r in range(CMP_STRIDE):
            x = xs_ref[pl.ds(r, n, stride=CMP_STRIDE), :].astype(BF16)
            acc = acc + jnp.dot(x, wc_ref[r], preferred_element_type=F32)
        p_ref[0] = acc


def _nsa_cmp_weights(w1):
    w = w1.reshape(2, 2, CMP_STRIDE, HEAD_DIM, 128)
    z = jnp.zeros((CMP_STRIDE, HEAD_DIM, 128), w1.dtype)
    top = jnp.concatenate([w[0, 0], z, w[0, 1], z], axis=-1)
    bot = jnp.concatenate([z, w[1, 0], z, w[1, 1]], axis=-1)
    return jnp.concatenate([top, bot], axis=1).astype(BF16)


def nsa_prep(wc, rows=None, cache=None, layer=None, page_table=None):
    if cache is None:
        b, t, _ = rows.shape
        kern = functools.partial(_nsa_prep_kernel, n_src=1, paged=False)
        return pl.pallas_call(
            kern, grid=(b, 1),
            in_specs=[pl.BlockSpec((None, t, 256), lambda i, j: (i, 0, 0)), pl.BlockSpec(wc.shape, lambda i, j: (0, 0, 0))],
            out_specs=[pl.BlockSpec((1, t, 128), lambda i, j: (i, 0, 0)),
                       pl.BlockSpec((1, t // CMP_STRIDE, 512), lambda i, j: (i, 0, 0))],
            out_shape=[jax.ShapeDtypeStruct((b, t, 128), BF16), jax.ShapeDtypeStruct((b, t // CMP_STRIDE, 512), F32)],
            scratch_shapes=[pltpu.VMEM((t, 128), F32)],
            compiler_params=_cparams("parallel", "arbitrary"), name="nsa_prep",
        )(rows, wc)
    b, n_pages = page_table.shape
    L = n_pages * PAGE_SIZE
    npg = 8
    kern = functools.partial(_nsa_prep_kernel, n_src=npg, paged=True)
    grid_spec = pltpu.PrefetchScalarGridSpec(
        num_scalar_prefetch=1, grid=(b, n_pages // npg),
        in_specs=_page_specs(npg, 256, layer) + [pl.BlockSpec(wc.shape, lambda i, j, pt: (0, 0, 0))],
        out_specs=[pl.BlockSpec((1, npg * PAGE_SIZE, 128), lambda i, j, pt: (i, j, 0)),
                   pl.BlockSpec((1, L // CMP_STRIDE, 512), lambda i, j, pt: (i, 0, 0))],
        scratch_shapes=[pltpu.VMEM((L, 128), F32)])
    return pl.pallas_call(
        kern, grid_spec=grid_spec,
        out_shape=[jax.ShapeDtypeStruct((b, L, 128), BF16), jax.ShapeDtypeStruct((b, L // CMP_STRIDE, 512), F32)],
        compiler_params=_cparams("parallel", "arbitrary"), name="nsa_prep_paged",
    )(page_table, *([cache] * npg), wc)


def _nsa_attend_kernel(*refs, tq, pos0, n_cmp, n_sel, n_chunks, has_new, t_total):
    (qraw_ref, qrot_ref, gate_ref, p_ref, cvec_ref, w2_ref, ov_ref, e0_ref, slc_ref, win_ref) = refs[:10]
    if has_new:
        newrow_ref, newwin_ref = refs[10:12]
        o_ref, kcvc_ref = refs[12:]
    else:
        o_ref, kcvc_ref = refs[10:]
    qt = pl.program_id(1)
    scale = HEAD_DIM ** -0.5
    r = NSA_HEADS * tq
    ncp = p_ref.shape[1]

    @pl.when(qt == 0)
    def _():
        p = p_ref[0]
        hid = p[:, 0:256] + pltpu.roll(p[:, 256:512], ncp - 1, 0) + cvec_ref[...]
        kcvc_ref[...] = jnp.dot(jax.nn.gelu(hid).astype(BF16), w2_ref[...], preferred_element_type=F32).astype(BF16)

    q0 = qt * tq
    p_row = pos0 + q0 + lax.broadcasted_iota(jnp.int32, (r, 1), 0) % tq
    p_q = pos0 + q0 + lax.broadcasted_iota(jnp.int32, (tq, 1), 0)
    q_raw = _stack_heads(qraw_ref[...], 128).astype(BF16)
    q_rot = _stack_heads(qrot_ref[...], 128).astype(BF16)
    kcvc = kcvc_ref[...]

    n_idx = lax.broadcasted_iota(jnp.int32, (r, ncp), 1)
    cmask = (n_idx * CMP_STRIDE + (CMP_BLOCK - 1) <= p_row) & (n_idx < n_cmp)
    (ec,), inv = _softmax_parts([_dot_nt(q_raw, kcvc) * scale], [cmask])
    pc = ec * inv
    o_cmp = jnp.dot(pc.astype(BF16), kcvc, preferred_element_type=F32)[:, HEAD_DIM:]
    imp_h = _dot3(pc, ov_ref[...])
    imp = functools.reduce(jnp.add, [imp_h[h * tq:(h + 1) * tq] for h in range(NSA_HEADS)])

    nsl = imp.shape[1]
    lane = lax.broadcasted_iota(jnp.int32, (tq, nsl), 1)
    jblk = (lane // 128) * (KEY_CHUNK // SEL_BLOCK) + lane % 128
    lane_ok = (lane % 128 < KEY_CHUNK // SEL_BLOCK) & (jblk < n_sel)
    cur = p_q // SEL_BLOCK
    forced = (jblk == 0) | (jblk == cur)
    score = jnp.where(forced, FORCE_SCORE, jnp.where(jblk < cur, imp, NEG_INF))
    score = jnp.where(lane_ok, score, NEG_INF)
    jf = jnp.where(lane_ok, jblk, 1 << 20).astype(F32)
    sel = _topk_select(score, jf, min(SEL_TOPK, n_sel))
    sel_r = jnp.concatenate([sel] * NSA_HEADS, axis=0).astype(BF16)

    scores, masks, vals = [], [], []
    klen = e0_ref.shape[1]
    kl = lax.broadcasted_iota(jnp.int32, (r, klen), 1)
    for c in range(n_chunks):
        kv = slc_ref[0, c * klen:(c + 1) * klen, :]
        picked = jnp.dot(sel_r[:, c * 128:(c + 1) * 128], e0_ref[...], preferred_element_type=F32) > 0.5
        scores.append(_dot_nt(q_rot, kv) * scale)
        masks.append(picked & (kl + c * klen <= p_row))
        vals.append(kv)
    if has_new:
        kv = newrow_ref[:, 128:256].astype(BF16)
        scores.append(_dot_nt(q_rot, kv) * scale)
        masks.append(pos0 + lax.broadcasted_iota(jnp.int32, (r, kv.shape[0]), 1) <= p_row)
        vals.append(kv)
    es, inv = _softmax_parts(scores, masks)
    o_slc = functools.reduce(jnp.add, [jnp.dot(e.astype(BF16), v, preferred_element_type=F32) for e, v in zip(es, vals)])
    o_slc = (o_slc * inv)[:, HEAD_DIM:]

    if has_new:
        wl = win_ref.shape[1]
        segs = [(win_ref[0].astype(BF16), pos0 - wl), (newwin_ref[...].astype(BF16), pos0)]
    else:
        span = min(t_total, WINDOW + tq)
        start = pl.multiple_of(jnp.clip(q0 - WINDOW, 0, t_total - span), tq)
        segs = [(win_ref[0, pl.ds(start, span), :].astype(BF16), pos0 + start)]
    scores, masks = [], []
    for kv, kp0 in segs:
        diff = p_row - (kp0 + lax.broadcasted_iota(jnp.int32, (r, kv.shape[0]), 1))
        scores.append(_dot_nt(q_rot, kv) * scale)
        masks.append((diff >= 0) & (diff < WINDOW) & (p_row - diff >= 0))
    es, inv = _softmax_parts(scores, masks)
    o_win = functools.reduce(jnp.add, [jnp.dot(e.astype(BF16), kv, preferred_element_type=F32) for e, (kv, _) in zip(es, segs)])
    o_win = (o_win * inv)[:, HEAD_DIM:]

    g = gate_ref[...]
    outs = []
    for h in range(NSA_HEADS):
        rows = slice(h * tq, (h + 1) * tq)
        outs.append(g[:, h:h + 1] * o_cmp[rows] + g[:, 4 + h:5 + h] * o_slc[rows] + g[:, 8 + h:9 + h] * o_win[rows])
    o_ref[...] = jnp.concatenate(outs, axis=1)


def _nsa_overlap(ncp, n_cmp, n_sel, nsl):
    per = KEY_CHUNK // SEL_BLOCK
    ov = np.zeros((ncp, nsl), np.float32)
    cs = np.arange(n_cmp) * CMP_STRIDE
    js = np.arange(n_sel) * SEL_BLOCK
    hit = (cs[:, None] < js[None, :] + SEL_BLOCK) & (cs[:, None] + CMP_BLOCK > js[None, :])
    lanes = (np.arange(n_sel) // per) * 128 + np.arange(n_sel) % per
    ov[:n_cmp, lanes] = hit
    return jnp.asarray(ov, dtype=BF16)


def nsa_attend(nq_raw, nq_rot, gates, pproj, cvec, w2bd, slc, win, b, t, pos0, new_rows=None, new_win=None):
    has_new = new_rows is not None
    tq = _pick_tile(t, 128)
    nq = t // tq
    l_slc = slc.shape[1]
    l_ctx = l_slc + (t if has_new else 0)
    assert l_slc % KEY_CHUNK == 0 or l_slc < KEY_CHUNK
    assert not has_new or (t <= SEL_BLOCK and l_slc % KEY_CHUNK == 0 and t < CMP_STRIDE)
    n_chunks = max(l_slc // KEY_CHUNK, 1)
    n_cmp = (l_ctx - CMP_BLOCK) // CMP_STRIDE + 1
    n_sel = -(-l_ctx // SEL_BLOCK)
    per = KEY_CHUNK // SEL_BLOCK
    nsl = -(-n_sel // per) * 128
    ncp = pproj.shape[1]
    ov = _nsa_overlap(ncp, n_cmp, n_sel, nsl)
    e0 = _block_expand_matrix(SEL_BLOCK)
    if l_slc < KEY_CHUNK:
        e0 = e0[:, :l_slc]
    kern = functools.partial(_nsa_attend_kernel, tq=tq, pos0=pos0, n_cmp=n_cmp, n_sel=n_sel, n_chunks=n_chunks,
                             has_new=has_new, t_total=t)
    tile = lambda w: pl.BlockSpec((tq, w), lambda i, j: (i * nq + j, 0))
    const = lambda a: pl.BlockSpec(a.shape, lambda i, j: (0,) * a.ndim)
    batch = lambda a: pl.BlockSpec((1,) + a.shape[1:], lambda i, j: (i,) + (0,) * (a.ndim - 1))
    in_specs = [tile(256), tile(256), tile(128), batch(pproj), const(cvec), const(w2bd), const(ov), const(e0),
                batch(slc), batch(win)]
    args = [nq_raw, nq_rot, gates, pproj, cvec, w2bd, ov, e0, slc, win]
    if has_new:
        in_specs += [tile(256), tile(128)]
        args += [new_rows, new_win]
    return pl.pallas_call(
        kern, grid=(b, nq), in_specs=in_specs, out_specs=tile(256),
        out_shape=jax.ShapeDtypeStruct((b * t, GROUP_WIDTH), F32),
        scratch_shapes=[pltpu.VMEM((ncp, 128), BF16)],
        compiler_params=_cparams("parallel", "arbitrary"), name="nsa_attend",
    )(*args)


MOBA_PER_CHUNK = KEY_CHUNK // MOBA_BLOCK


def _moba_prep_kernel(*refs, n_src, paged):
    if paged:
        refs = refs[1:]
    srcs, avg_ref, kv_ref, mean_ref = refs[:n_src], refs[n_src], refs[n_src + 1], refs[n_src + 2]
    rows = srcs[0].shape[0]
    for k, src in enumerate(srcs):
        kv_ref[0, k * rows:(k + 1) * rows, :] = src[...].astype(BF16)
    mean_ref[0] = jnp.dot(avg_ref[...], kv_ref[0, :, 0:256], preferred_element_type=F32)


def moba_prep(rows=None, cache=None, layer=None, page_table=None):
    if cache is None:
        b, t, _ = rows.shape
        klen = min(KEY_CHUNK, t)
        n_chunks, n_src = t // klen, 1
        in_specs = [pl.BlockSpec((None, klen, 512), lambda i, j: (i, j, 0))]
        args, grid_kw, nsp = (rows,), {}, 0
    else:
        b, n_pages = page_table.shape
        n_src = min(KEY_CHUNK // PAGE_SIZE, n_pages)
        klen = n_src * PAGE_SIZE
        n_chunks = n_pages // n_src
        in_specs = _page_specs(n_src, 512, layer)
        args, nsp = (page_table,) + (cache,) * n_src, 1
    avg = (_block_expand_matrix(MOBA_BLOCK)[:, :klen].astype(F32) / MOBA_BLOCK).astype(BF16)
    imap = (lambda i, j, pt: (i, j, 0)) if nsp else (lambda i, j: (i, j, 0))
    cmap = (lambda i, j, pt: (0, 0)) if nsp else (lambda i, j: (0, 0))
    grid_spec = pltpu.PrefetchScalarGridSpec(
        num_scalar_prefetch=nsp, grid=(b, n_chunks),
        in_specs=in_specs + [pl.BlockSpec(avg.shape, cmap)],
        out_specs=[pl.BlockSpec((1, klen, 512), imap), pl.BlockSpec((1, 128, 256), imap)])
    return pl.pallas_call(
        functools.partial(_moba_prep_kernel, n_src=n_src, paged=bool(nsp)), grid_spec=grid_spec,
        out_shape=[jax.ShapeDtypeStruct((b, n_chunks * klen, 512), BF16),
                   jax.ShapeDtypeStruct((b, n_chunks * 128, 256), F32)],
        compiler_params=_cparams("parallel", "arbitrary"), name="moba_prep",
    )(*args, avg)


def _moba_attend_kernel(*refs, tq, pos0, n_cols, n_chunks, has_new):
    mq_ref, mean_ref, e0_ref, kv_ref = refs[:4]
    if has_new:
        new_ref = refs[4]
        o_ref, sel_ref, m_ref, l_ref, acc_ref = refs[5:]
    else:
        o_ref, sel_ref, m_ref, l_ref, acc_ref = refs[4:]
    qt, c = pl.program_id(1), pl.program_id(2)
    scale = HEAD_DIM ** -0.5
    r2 = 2 * tq
    klen = e0_ref.shape[1]
    q = mq_ref[...]
    q0 = pos0 + qt * tq
    p_row = q0 + lax.broadcasted_iota(jnp.int32, (r2, 1), 0) % tq

    @pl.when(c == 0)
    def _():
        means = mean_ref[0].astype(BF16)
        nsl = n_chunks * 128
        lane = lax.broadcasted_iota(jnp.int32, (tq, nsl), 1)
        jblk = (lane // 128) * MOBA_PER_CHUNK + lane % 128
        lane_ok = (lane % 128 < MOBA_PER_CHUNK) & (jblk < n_cols)
        own = (q0 + lax.broadcasted_iota(jnp.int32, (tq, 1), 0)) // MOBA_BLOCK
        jf = jnp.where(lane_ok, jblk, 1 << 20).astype(F32)
        head_of_lane = lax.broadcasted_iota(jnp.int32, q.shape, 1) // HEAD_DIM
        for h in range(MOBA_HEADS):
            qh = jnp.where(head_of_lane == h, q, 0.0).astype(BF16)
            score = jnp.where(lane_ok & (jblk < own), _dot_nt(qh, means), NEG_INF)
            sel = _topk_select(score, jf, MOBA_TOPK)
            for cc in range(n_chunks):
                sel_ref[cc, h * tq:(h + 1) * tq, :] = sel[:, cc * 128:(cc + 1) * 128]
        m_ref[...] = jnp.full(m_ref.shape, NEG_INF, F32)
        l_ref[...] = jnp.zeros(l_ref.shape, F32)
        acc_ref[...] = jnp.zeros(acc_ref.shape, F32)

    lane128 = lax.broadcasted_iota(jnp.int32, (tq, 128), 1)

    def pair_q(pr):
        x = q[:, 128 * pr:128 * (pr + 1)]
        return jnp.concatenate([jnp.where(lane128 < HEAD_DIM, x, 0.0), jnp.where(lane128 >= HEAD_DIM, x, 0.0)],
                               axis=0).astype(BF16)

    def update(pr, s, mask, v):
        m_old = m_ref[pr][:, 0:1]
        m_new = jnp.maximum(m_old, jnp.max(jnp.where(mask, s, NEG_INF), axis=-1, keepdims=True))
        m_safe = jnp.where(m_new == NEG_INF, 0.0, m_new)
        alpha = jnp.exp(m_old - m_safe)
        e = jnp.where(mask, jnp.exp(s - m_safe), 0.0)
        l_ref[pr] = alpha * l_ref[pr] + jnp.sum(e, axis=-1, keepdims=True)
        acc_ref[pr] = alpha * acc_ref[pr] + jnp.dot(e.astype(BF16), v, preferred_element_type=F32)
        m_ref[pr] = jnp.broadcast_to(m_new, (r2, 128))

    own_row = p_row // MOBA_BLOCK
    kpos = c * klen + lax.broadcasted_iota(jnp.int32, (r2, klen), 1)
    for pr in range(2):
        picked = jnp.dot(sel_ref[c, 2 * pr * tq:(2 * pr + 2) * tq, :].astype(BF16), e0_ref[...],
                         preferred_element_type=F32) > 0.5
        mask = (picked | (kpos // MOBA_BLOCK == own_row)) & (kpos <= p_row)
        s = _dot_nt(pair_q(pr), kv_ref[0, :, 128 * pr:128 * (pr + 1)]) * scale
        update(pr, s, mask, kv_ref[0, :, 256 + 128 * pr:256 + 128 * (pr + 1)])

    @pl.when(c == n_chunks - 1)
    def _():
        outs = []
        for pr in range(2):
            if has_new:
                t_new = new_ref.shape[0]
                kp = pos0 + lax.broadcasted_iota(jnp.int32, (r2, t_new), 1)
                mask = (kp // MOBA_BLOCK == own_row) & (kp <= p_row)
                s = _dot_nt(pair_q(pr), new_ref[:, 128 * pr:128 * (pr + 1)].astype(BF16)) * scale
                update(pr, s, mask, new_ref[:, 256 + 128 * pr:256 + 128 * (pr + 1)].astype(BF16))
            o = acc_ref[pr] / jnp.maximum(l_ref[pr][:, 0:1], 1e-30)
            outs += [o[:tq, :HEAD_DIM], o[tq:, HEAD_DIM:]]
        o_ref[...] = jnp.concatenate(outs, axis=1)


def moba_attend(mq, means, kv, b, t, pos0, new_rows=None):
    has_new = new_rows is not None
    tq = _pick_tile(t, 128)
    nq = t // tq
    l_kv = kv.shape[1]
    klen = min(KEY_CHUNK, l_kv)
    n_chunks = l_kv // klen
    n_full = (l_kv + (t if has_new else 0)) // MOBA_BLOCK
    assert n_full >= MOBA_TOPK and l_kv % klen == 0 and klen % MOBA_BLOCK == 0
    assert not has_new or t <= MOBA_BLOCK
    e0 = _block_expand_matrix(MOBA_BLOCK)[:, :klen]
    kern = functools.partial(_moba_attend_kernel, tq=tq, pos0=pos0, n_cols=n_full, n_chunks=n_chunks, has_new=has_new)
    tile = lambda w: pl.BlockSpec((tq, w), lambda i, j, c: (i * nq + j, 0))
    in_specs = [tile(256), pl.BlockSpec((1, n_chunks * 128, 256), lambda i, j, c: (i, 0, 0)),
                pl.BlockSpec(e0.shape, lambda i, j, c: (0, 0)), pl.BlockSpec((1, klen, 512), lambda i, j, c: (i, c, 0))]
    args = [mq, means, e0, kv]
    if has_new:
        in_specs.append(tile(512))
        args.append(new_rows)
    return pl.pallas_call(
        kern, grid=(b, nq, n_chunks), in_specs=in_specs, out_specs=tile(256),
        out_shape=jax.ShapeDtypeStruct((b * t, GROUP_WIDTH), F32),
        scratch_shapes=[pltpu.VMEM((n_chunks, MOBA_HEADS * tq, 128), F32), pltpu.VMEM((2, 2 * tq, 128), F32),
                        pltpu.VMEM((2, 2 * tq, 128), F32), pltpu.VMEM((2, 2 * tq, 128), F32)],
        compiler_params=_cparams("parallel", "arbitrary", "arbitrary"), name="moba_attend",
    )(*args)


def _layer(h, p_emb, b, t, pos0, lw, past, final_gain, final):
    prompt = past is None
    pos = pos0 + jnp.arange(t, dtype=jnp.int32)
    h = ffn_block(h, lw['norm_ffn1'], lw['ffn1_w_in'], lw['ffn1_w_out'])
    (gla, loga, nq_raw, nq_rot, nsa_rows, win_rows, gates, su, mq, moba_rows) = mixin_block(
        h, lw['norm_mix'], lw['wcat'], _rope_table(pos), lw['gla_gate_w'], lw['gla_gate_b'], t)

    s0 = jnp.zeros((b, GLA_HEADS * GLA_DK, GLA_DV), F32) if prompt else past['gla'].reshape(b, -1, GLA_DV)
    o_gla, gla_state = gla_block(gla, loga, s0, lw['gla_norm'], b, GLA_CHUNK if prompt else t)
    gla_state = gla_state.reshape(b, GLA_HEADS, GLA_DK, GLA_DV)

    rows3 = nsa_rows.reshape(b, t, 4 * HEAD_DIM)
    win3 = win_rows.reshape(b, t, 2 * HEAD_DIM)
    if prompt:
        slc, pproj = nsa_prep(lw['nsa_wc'], rows=rows3)
        o_nsa = nsa_attend(nq_raw, nq_rot, gates, pproj, lw['nsa_cvec'], lw['nsa_w2bd'], slc, win3, b, t, 0)
        new_win = win3[:, -min(WINDOW, t):]
    else:
        slc, pproj = nsa_prep(lw['nsa_wc'], cache=past['nsa_cache'], layer=past['layer'], page_table=past['page_table'])
        wb = past['win'].reshape(b, -1, 2 * HEAD_DIM)
        o_nsa = nsa_attend(nq_raw, nq_rot, gates, pproj, lw['nsa_cvec'], lw['nsa_w2bd'], slc, wb, b, t, pos0,
                           new_rows=nsa_rows, new_win=win_rows)
        new_win = jnp.concatenate([wb, win3], axis=1)[:, -wb.shape[1]:]

    if prompt:
        h0 = jnp.zeros((b, 1, 2 * S5_LANES), F32)
    else:
        h0 = jnp.concatenate([past['s5_re'].reshape(b, 1, -1), past['s5_im'].reshape(b, 1, -1)], axis=-1)
    o_s5, s5_re, s5_im = s5_block(su, h0, lw['s5_pw'], lw['s5_bcat'], lw['s5_ccat'], lw['s5_d'],
                                  lw['s5_glu_w'], lw['s5_glu_b'], b)

    mrows = moba_rows.reshape(b, t, 2, MOBA_HEADS, HEAD_DIM)
    if prompt:
        mkv, means = moba_prep(rows=moba_rows.reshape(b, t, 512))
        o_moba = moba_attend(mq, means, mkv, b, t, 0)
    else:
        mkv, means = moba_prep(cache=past['moba_cache'], layer=past['layer'], page_table=past['page_table'])
        o_moba = moba_attend(mq, means, mkv, b, t, pos0, new_rows=moba_rows)

    h = mixout_block(h, (o_gla, o_nsa, o_s5, o_moba), lw['w_mix_out'])
    h = ffn_block(h, lw['norm_ffn2'], lw['ffn2_w_in'], lw['ffn2_w_out'])
    h = ple_block(h, p_emb, lw['norm_ple'], lw['ple_w'], lw['ple_gate_w'], final_gain, final)
    state = {'nsa': nsa_rows.reshape(b, t, 4, HEAD_DIM), 'moba': mrows,
             'win': new_win.reshape(b, -1, 2, HEAD_DIM), 'gla': gla_state, 's5_re': s5_re, 's5_im': s5_im}
    return h, state


def kernel(x_prompt, x_sample, p_prompt, p_sample, cache_nsa_kv, cache_moba_kv, state_nsa_win, state_gla,
           state_s5_re, state_s5_im, page_table, norm_ffn1, ffn1_w_in, ffn1_w_out, norm_mix, w_mix_in, w_mix_out,
           gla_gate_w, gla_gate_b, gla_norm, nsa_cmp_pe, nsa_cmp_w1, nsa_cmp_w2, s5_a_re, s5_a_im, s5_b_re, s5_b_im,
           s5_c_re, s5_c_im, s5_d, s5_log_dt, s5_glu_w, s5_glu_b, norm_ffn2, ffn2_w_in, ffn2_w_out, norm_ple, ple_w,
           ple_gate_w, norm_final):
    bp, tp, d = x_prompt.shape
    bs, ts, _ = x_sample.shape
    depth = p_prompt.shape[0]
    past_len = page_table.shape[1] * PAGE_SIZE
    params = {'norm_ffn1': norm_ffn1, 'norm_mix': norm_mix, 'gla_gate_w': gla_gate_w, 'gla_gate_b': gla_gate_b,
              'gla_norm': gla_norm, 'nsa_cmp_pe': nsa_cmp_pe, 'nsa_cmp_w1': nsa_cmp_w1, 'nsa_cmp_w2': nsa_cmp_w2,
              's5_a_re': s5_a_re, 's5_a_im': s5_a_im, 's5_b_re': s5_b_re, 's5_b_im': s5_b_im, 's5_c_re': s5_c_re,
              's5_c_im': s5_c_im, 's5_d': s5_d, 's5_log_dt': s5_log_dt, 's5_glu_w': s5_glu_w, 's5_glu_b': s5_glu_b,
              'norm_ffn2': norm_ffn2, 'norm_ple': norm_ple}
    hp = x_prompt.reshape(bp * tp, d)
    hs = x_sample.reshape(bs * ts, d)
    nsa_cache = cache_nsa_kv.reshape(cache_nsa_kv.shape[:3] + (4 * HEAD_DIM,))
    moba_cache = cache_moba_kv.reshape(cache_moba_kv.shape[:3] + (2 * MOBA_HEADS * HEAD_DIM,))
    st_p, st_s = [], []
    for i in range(depth):
        lw = {k: v[i] for k, v in params.items()}
        lw['wcat'] = _mixin_weights(w_mix_in[i])
        for name, w in (('ffn1_w_in', ffn1_w_in), ('ffn1_w_out', ffn1_w_out), ('ffn2_w_in', ffn2_w_in),
                        ('ffn2_w_out', ffn2_w_out), ('w_mix_out', w_mix_out), ('ple_w', ple_w),
                        ('ple_gate_w', ple_gate_w)):
            lw[name] = w[i].astype(BF16)
        lw['s5_pw'], lw['s5_bcat'], lw['s5_ccat'] = _s5_params(s5_a_re[i], s5_a_im[i], s5_b_re[i], s5_b_im[i],
                                                               s5_c_re[i], s5_c_im[i], s5_log_dt[i])
        lw['s5_glu_w'] = s5_glu_w[i].astype(BF16)
        lw['nsa_wc'] = _nsa_cmp_weights(nsa_cmp_w1[i])
        pe, w1, w2 = nsa_cmp_pe[i], nsa_cmp_w1[i], nsa_cmp_w2[i]
        lw['nsa_cvec'] = jnp.concatenate([jnp.dot(pe[s].reshape(1, -1), w1[s], precision=HIGHEST) for s in range(2)], axis=1)
        zw = jnp.zeros_like(w2[0])
        lw['nsa_w2bd'] = jnp.concatenate([jnp.concatenate([w2[0], zw], axis=1),
                                          jnp.concatenate([zw, w2[1]], axis=1)], axis=0).astype(BF16)
        final = i == depth - 1
        hp, sp = _layer(hp, p_prompt[i].reshape(bp * tp, -1), bp, tp, 0, lw, None, norm_final, final)
        past = {'nsa_cache': nsa_cache, 'moba_cache': moba_cache, 'layer': i, 'page_table': page_table,
                'win': state_nsa_win[i], 'gla': state_gla[i], 's5_re': state_s5_re[i], 's5_im': state_s5_im[i]}
        hs, ss = _layer(hs, p_sample[i].reshape(bs * ts, -1), bs, ts, past_len, lw, past, norm_final, final)
        st_p.append(sp)
        st_s.append(ss)

    def stack(states, name):
        return jnp.stack([s[name] for s in states], axis=0)

    return (hp.reshape(bp, tp, d), hs.reshape(bs, ts, d),
            stack(st_p, 'nsa'), stack(st_s, 'nsa'), stack(st_p, 'moba'), stack(st_s, 'moba'),
            stack(st_p, 'win'), stack(st_s, 'win'), stack(st_p, 'gla'), stack(st_s, 'gla'),
            stack(st_p, 's5_re'), stack(st_s, 's5_re'), stack(st_p, 's5_im'), stack(st_s, 's5_im'))
```

```python
import functools
import math

import numpy as np
import jax
import jax.numpy as jnp
from jax import lax
from jax.experimental import pallas as pl
from jax.experimental.pallas import tpu as pltpu

F32 = jnp.float32
BF16 = jnp.bfloat16

HEAD_DIM = 64
ROPE_THETA = 10000.0
RMS_EPS = 1e-6
PAGE_SIZE = 128
GLA_HEADS = 4
GLA_DK = 32
GLA_DV = 64
GLA_GATE_RANK = 16
GLA_TAU = 16.0
GLA_CHUNK = 32
NSA_HEADS = 4
CMP_BLOCK = 32
CMP_STRIDE = 16
SEL_BLOCK = 64
SEL_TOPK = 16
WINDOW = 512
FORCE_SCORE = 1e4
S5_GROUP_SIZE = 16
S5_GROUPS = 16
S5_STATE = 64
MOBA_HEADS = 4
MOBA_BLOCK = 256
MOBA_TOPK = 3
GROUP_WIDTH = 256

VMEM_LIMIT_BYTES = 48 * 1024 * 1024


def _cparams(*sem):
    return pltpu.CompilerParams(dimension_semantics=sem, vmem_limit_bytes=VMEM_LIMIT_BYTES)


def _rms(x, g):
    return x * lax.rsqrt(jnp.mean(x * x, axis=-1, keepdims=True) + RMS_EPS) * g


def _pick_tile(m, pref):
    t = min(m, pref)
    while m % t:
        t //= 2
    return t


def _ffn_kernel(x_ref, g_ref, wg_ref, wu_ref, wo_ref, o_ref, xn_ref, acc_ref):
    j = pl.program_id(1)

    @pl.when(j == 0)
    def _():
        xn_ref[...] = _rms(x_ref[...], g_ref[...]).astype(BF16)
        acc_ref[...] = jnp.zeros_like(acc_ref)

    xn = xn_ref[...]
    g = jnp.dot(xn, wg_ref[...], preferred_element_type=F32)
    u = jnp.dot(xn, wu_ref[...], preferred_element_type=F32)
    act = (g * jax.nn.sigmoid(g) * u).astype(BF16)
    acc_ref[...] += jnp.dot(act, wo_ref[...], preferred_element_type=F32)

    @pl.when(j == pl.num_programs(1) - 1)
    def _():
        o_ref[...] = x_ref[...] + 0.5 * acc_ref[...]


def ffn_block(x, gain, w_in, w_out):
    m, d = x.shape
    f = w_out.shape[0]
    tm = _pick_tile(m, 1024)
    tf = 256
    nf = f // tf
    return pl.pallas_call(
        _ffn_kernel,
        grid=(m // tm, nf),
        in_specs=[
            pl.BlockSpec((tm, d), lambda i, j: (i, 0)),
            pl.BlockSpec((1, d), lambda i, j: (0, 0)),
            pl.BlockSpec((d, tf), lambda i, j: (0, j)),
            pl.BlockSpec((d, tf), lambda i, j: (0, j + nf)),
            pl.BlockSpec((tf, d), lambda i, j: (j, 0)),
        ],
        out_specs=pl.BlockSpec((tm, d), lambda i, j: (i, 0)),
        out_shape=jax.ShapeDtypeStruct((m, d), F32),
        scratch_shapes=[pltpu.VMEM((tm, d), BF16), pltpu.VMEM((tm, d), F32)],
        compiler_params=_cparams("parallel", "arbitrary"),
        name="ffn_block",
    )(x, gain.reshape(1, d), w_in, w_in, w_out)


MIXIN_OUT_WIDTHS = (768, 128, 256, 256, 256, 128, 128, 256, 256, 512)


def _mixin_kernel(h_ref, g_ref, w_ref, tab_ref, gw_ref, gb_ref,
                  gla_o, loga_o, nqraw_o, nqrot_o, nsa_o, win_o, gate_o, su_o, mq_o, moba_o):
    xn = _rms(h_ref[...], g_ref[...]).astype(BF16)

    def mm(a, b):
        return jnp.dot(xn, w_ref[:, a:b], preferred_element_type=F32)

    c256, s256 = tab_ref[:, 0:256], tab_ref[:, 256:512]
    gla_o[...] = mm(0, 768)
    z = mm(768, 896)
    la = jnp.dot(z, gw_ref[...], preferred_element_type=F32) + gb_ref[...]
    loga_o[...] = (jnp.minimum(la, 0.0) - jnp.log(1.0 + jnp.exp(-jnp.abs(la)))) * (1.0 / GLA_TAU)
    nq = mm(896, 1152)
    nqraw_o[...] = nq
    nqrot_o[...] = nq * c256 + mm(1152, 1408) * s256
    nsa_o[...] = mm(1408, 1664) * tab_ref[:, 512:768] + mm(1792, 2048) * tab_ref[:, 768:1024]
    win_o[...] = mm(1664, 1792) * tab_ref[:, 1024:1152] + mm(2048, 2176) * tab_ref[:, 1152:1280]
    gate_o[...] = jax.nn.sigmoid(mm(2176, 2304))
    su_o[...] = mm(2304, 2560)
    mq_o[...] = mm(2560, 2816) * c256 + mm(2816, 3072) * s256
    moba_o[:, 0:256] = mm(3072, 3328) * c256 + mm(3328, 3584) * s256
    moba_o[:, 256:512] = mm(3584, 3840)


def _rot_cols(w):
    d, n = w.shape
    w = w.reshape(d, n // HEAD_DIM, 2, HEAD_DIM // 2)
    return jnp.stack([-w[:, :, 1], w[:, :, 0]], axis=2).reshape(d, n)


def _mixin_weights(w):
    d = w.shape[0]
    z = lambda n: jnp.zeros((d, n), w.dtype)
    gq, gk, gv, gz, gog = w[:, 0:128], w[:, 128:256], w[:, 256:512], w[:, 512:528], w[:, 528:784]
    nq, nkv, ng, su = w[:, 784:1040], w[:, 1040:1424], w[:, 1424:1436], w[:, 1436:1692]
    mq, mk, mv = w[:, 1692:1948], w[:, 1948:2204], w[:, 2204:2460]
    r_nsa = jnp.concatenate([z(128), _rot_cols(nkv[:, 128:192]), z(64)], axis=1)
    r_win = jnp.concatenate([_rot_cols(nkv[:, 256:320]), z(64)], axis=1)
    cat = jnp.concatenate([gq, gk, gv, gog, gz, z(112), nq, _rot_cols(nq), nkv, r_nsa, r_win, ng, z(116), su,
                           mq, _rot_cols(mq), mk, _rot_cols(mk), mv], axis=1)
    return cat.astype(BF16)


def _rope_table(pos):
    half = HEAD_DIM // 2
    inv = ROPE_THETA ** (-jnp.arange(half, dtype=F32) / half)
    ang = pos.astype(F32)[:, None] * inv
    c, s = jnp.cos(ang), jnp.sin(ang)
    c64, s64 = jnp.concatenate([c, c], -1), jnp.concatenate([s, s], -1)
    one, zero = jnp.ones_like(c64), jnp.zeros_like(c64)
    return jnp.concatenate([c64, c64, c64, c64, s64, s64, s64, s64, one, one, c64, one, zero, zero, s64, zero,
                            c64, one, s64, zero], axis=-1)


def mixin_block(h, gain, wcat, table, gate_w, gate_b, t_len):
    m, d = h.shape
    tm = _pick_tile(m, 512)
    if t_len % tm == 0:
        ntab = t_len // tm
        tab_map = lambda i: (i % ntab, 0)
    else:
        table = jnp.tile(table, (tm // t_len, 1))
        tab_map = lambda i: (0, 0)
    gw = jnp.zeros((128, 128), F32).at[:GLA_GATE_RANK].set(gate_w)
    return pl.pallas_call(
        _mixin_kernel,
        grid=(m // tm,),
        in_specs=[
            pl.BlockSpec((tm, d), lambda i: (i, 0)),
            pl.BlockSpec((1, d), lambda i: (0, 0)),
            pl.BlockSpec(wcat.shape, lambda i: (0, 0)),
            pl.BlockSpec((tm, table.shape[1]), tab_map),
            pl.BlockSpec((128, 128), lambda i: (0, 0)),
            pl.BlockSpec((1, 128), lambda i: (0, 0)),
        ],
        out_specs=[pl.BlockSpec((tm, n), lambda i: (i, 0)) for n in MIXIN_OUT_WIDTHS],
        out_shape=[jax.ShapeDtypeStruct((m, n), F32) for n in MIXIN_OUT_WIDTHS],
        compiler_params=_cparams("parallel"),
        name="mixin_block",
    )(h, gain.reshape(1, d), wcat, table, gw, gate_b.reshape(1, 128))


def _mixout_kernel(h_ref, a_ref, b_ref, c_ref, d_ref, w_ref, o_ref):
    acc = h_ref[...]
    for k, r in enumerate((a_ref, b_ref, c_ref, d_ref)):
        acc = acc + jnp.dot(r[...].astype(BF16), w_ref[k * GROUP_WIDTH:(k + 1) * GROUP_WIDTH, :],
                            preferred_element_type=F32)
    o_ref[...] = acc


def mixout_block(h, parts, w):
    m, d = h.shape
    tm = _pick_tile(m, 512)
    return pl.pallas_call(
        _mixout_kernel,
        grid=(m // tm,),
        in_specs=[pl.BlockSpec((tm, d), lambda i: (i, 0))]
        + [pl.BlockSpec((tm, GROUP_WIDTH), lambda i: (i, 0)) for _ in range(4)]
        + [pl.BlockSpec((d, d), lambda i: (0, 0))],
        out_specs=pl.BlockSpec((tm, d), lambda i: (i, 0)),
        out_shape=jax.ShapeDtypeStruct((m, d), F32),
        compiler_params=_cparams("parallel"),
        name="mixout_block",
    )(h, *parts, w)


def _ple_kernel(h_ref, p_ref, g_ref, wp_ref, wg_ref, gf_ref, o_ref, *, final):
    h = h_ref[...]
    gate = jax.nn.sigmoid(jnp.dot(_rms(h, g_ref[...]).astype(BF16), wg_ref[...], preferred_element_type=F32))
    e = jnp.dot(p_ref[...].astype(BF16), wp_ref[...], preferred_element_type=F32)
    y = h + e * gate
    o_ref[...] = _rms(y, gf_ref[...]) if final else y


def ple_block(h, p, gain, ple_w, gate_w, final_gain, final):
    m, d = h.shape
    pd = p.shape[1]
    tm = _pick_tile(m, 512)
    return pl.pallas_call(
        functools.partial(_ple_kernel, final=final),
        grid=(m // tm,),
        in_specs=[
            pl.BlockSpec((tm, d), lambda i: (i, 0)),
            pl.BlockSpec((tm, pd), lambda i: (i, 0)),
            pl.BlockSpec((1, d), lambda i: (0, 0)),
            pl.BlockSpec((pd, d), lambda i: (0, 0)),
            pl.BlockSpec((d, d), lambda i: (0, 0)),
            pl.BlockSpec((1, d), lambda i: (0, 0)),
        ],
        out_specs=pl.BlockSpec((tm, d), lambda i: (i, 0)),
        out_shape=jax.ShapeDtypeStruct((m, d), F32),
        compiler_params=_cparams("parallel"),
        name="ple_block",
    )(h, p, gain.reshape(1, d), ple_w, gate_w, final_gain.reshape(1, d))


S5_LANES = S5_GROUPS * S5_STATE
S5_TILE = 8
HIGHEST = lax.Precision.HIGHEST


def _s5_kernel(u_ref, h0_ref, pw_ref, bcat_ref, ccat_ref, d_ref, gw_ref, gb_ref,
               o_ref, hre_ref, him_ref, bu_ref, h_ref, carry_ref):
    n = S5_LANES
    tb = u_ref.shape[0]

    @pl.when(pl.program_id(1) == 0)
    def _():
        carry_ref[...] = jnp.broadcast_to(h0_ref[0], (S5_TILE, 2 * n))

    u = u_ref[...]
    bu_ref[...] = jnp.dot(u.astype(BF16), bcat_ref[...], preferred_element_type=F32)

    def tile(k, carry):
        cr, ci = carry
        r0 = pl.multiple_of(k * S5_TILE, S5_TILE)
        xr = bu_ref[pl.ds(r0, S5_TILE), 0:n]
        xi = bu_ref[pl.ds(r0, S5_TILE), n:2 * n]
        for lvl, sh in enumerate((1, 2, 4)):
            pr = pw_ref[:, (2 * lvl) * n:(2 * lvl + 1) * n]
            pi = pw_ref[:, (2 * lvl + 1) * n:(2 * lvl + 2) * n]
            rr = pltpu.roll(xr, sh, 0)
            ri = pltpu.roll(xi, sh, 0)
            xr, xi = xr + pr * rr - pi * ri, xi + pr * ri + pi * rr
        pr, pi = pw_ref[:, 6 * n:7 * n], pw_ref[:, 7 * n:8 * n]
        xr, xi = xr + pr * cr - pi * ci, xi + pr * ci + pi * cr
        h_ref[pl.ds(r0, S5_TILE), 0:n] = xr
        h_ref[pl.ds(r0, S5_TILE), n:2 * n] = xi
        last_r = jnp.broadcast_to(xr[S5_TILE - 1:S5_TILE, :], (S5_TILE, n))
        last_i = jnp.broadcast_to(xi[S5_TILE - 1:S5_TILE, :], (S5_TILE, n))
        return last_r, last_i

    cr, ci = lax.fori_loop(0, tb // S5_TILE, tile, (carry_ref[:, 0:n], carry_ref[:, n:2 * n]),
                           unroll=min(4, tb // S5_TILE))
    carry_ref[:, 0:n] = cr
    carry_ref[:, n:2 * n] = ci
    hre_ref[0] = cr[0:1, :]
    him_ref[0] = ci[0:1, :]

    y = jnp.dot(h_ref[...].astype(BF16), ccat_ref[...], preferred_element_type=F32) + d_ref[...] * u
    g = jax.nn.gelu(y)
    gate = jnp.dot(g.astype(BF16), gw_ref[...], preferred_element_type=F32) + gb_ref[...]
    o_ref[...] = g * jax.nn.sigmoid(gate)


def _s5_params(a_re, a_im, b_re, b_im, c_re, c_im, log_dt):
    g, n = a_re.shape
    c = b_re.shape[-1]
    dt = jnp.exp(log_dt)[:, None]
    mag = jnp.exp(a_re * dt)
    abar_re, abar_im = mag * jnp.cos(a_im * dt), mag * jnp.sin(a_im * dt)
    den = a_re * a_re + a_im * a_im
    fr = ((abar_re - 1.0) * a_re + abar_im * a_im) / den
    fi = (abar_im * a_re - (abar_re - 1.0) * a_im) / den
    bbar_re = fr[..., None] * b_re - fi[..., None] * b_im
    bbar_im = fr[..., None] * b_im + fi[..., None] * b_re
    eye = jnp.eye(g, dtype=F32)
    blk_in = lambda m: jnp.einsum('gnc,gh->gchn', m, eye).reshape(g * c, g * n)
    blk_out = lambda m: jnp.einsum('gcn,gh->gnhc', m, eye).reshape(g * n, g * c)
    bcat = jnp.concatenate([blk_in(bbar_re), blk_in(bbar_im)], axis=1).astype(BF16)
    ccat = jnp.concatenate([blk_out(c_re), -blk_out(c_im)], axis=0).astype(BF16)
    ar, ai = abar_re.reshape(1, g * n), abar_im.reshape(1, g * n)
    pows = [(ar, ai)]
    for _ in range(S5_TILE - 1):
        pr, pi = pows[-1]
        pows.append((pr * ar - pi * ai, pr * ai + pi * ar))
    t = jnp.arange(S5_TILE)[:, None]
    cols = []
    for sh in (1, 2, 4):
        pr, pi = pows[sh - 1]
        cols += [jnp.where(t >= sh, pr, 0.0), jnp.where(t >= sh, pi, 0.0)]
    cols += [jnp.concatenate([p[0] for p in pows], axis=0), jnp.concatenate([p[1] for p in pows], axis=0)]
    return jnp.concatenate(cols, axis=1), bcat, ccat


def s5_block(su, h0, pw, bcat, ccat, dvec, glu_w, glu_b, b):
    m = su.shape[0]
    t = m // b
    tb = _pick_tile(t, 256)
    nt = t // tb
    n = S5_LANES
    out, hre, him = pl.pallas_call(
        _s5_kernel,
        grid=(b, nt),
        in_specs=[
            pl.BlockSpec((tb, GROUP_WIDTH), lambda i, j: (i * nt + j, 0)),
            pl.BlockSpec((1, 1, 2 * n), lambda i, j: (i, 0, 0)),
            pl.BlockSpec(pw.shape, lambda i, j: (0, 0)),
            pl.BlockSpec(bcat.shape, lambda i, j: (0, 0)),
            pl.BlockSpec(ccat.shape, lambda i, j: (0, 0)),
            pl.BlockSpec((1, GROUP_WIDTH), lambda i, j: (0, 0)),
            pl.BlockSpec((GROUP_WIDTH, GROUP_WIDTH), lambda i, j: (0, 0)),
            pl.BlockSpec((1, GROUP_WIDTH), lambda i, j: (0, 0)),
        ],
        out_specs=[
            pl.BlockSpec((tb, GROUP_WIDTH), lambda i, j: (i * nt + j, 0)),
            pl.BlockSpec((1, 1, n), lambda i, j: (i, 0, 0)),
            pl.BlockSpec((1, 1, n), lambda i, j: (i, 0, 0)),
        ],
        out_shape=[jax.ShapeDtypeStruct((m, GROUP_WIDTH), F32),
                   jax.ShapeDtypeStruct((b, 1, n), F32), jax.ShapeDtypeStruct((b, 1, n), F32)],
        scratch_shapes=[pltpu.VMEM((tb, 2 * n), F32), pltpu.VMEM((tb, 2 * n), F32), pltpu.VMEM((S5_TILE, 2 * n), F32)],
        compiler_params=_cparams("parallel", "arbitrary"),
        name="s5_block",
    )(su, h0, pw, bcat, ccat, dvec.reshape(1, -1), glu_w, glu_b.reshape(1, -1))
    return out, hre.reshape(b, S5_GROUPS, S5_STATE), him.reshape(b, S5_GROUPS, S5_STATE)


def _gla_consts(c):
    dk = GLA_DK
    tril = np.tril(np.ones((c, c), np.float32))
    lane_j = np.arange(c * dk) // dk
    diag = (np.arange(c)[:, None] == lane_j[None, :]).astype(np.float32)
    causal = (lane_j[None, :] <= np.arange(c)[:, None]).astype(np.float32)
    gsum = (lane_j[:, None] == np.arange(c)[None, :]).astype(np.float32)
    return jnp.asarray(tril), jnp.asarray(diag), jnp.asarray(causal), jnp.asarray(gsum, dtype=BF16)


def _gla_kernel(x_ref, la_ref, s0_ref, tril_ref, diag_ref, causal_ref, gsum_ref, gn_ref,
                o_ref, sfin_ref, s_ref, *, chunk):
    c, dk, dv = chunk, GLA_DK, GLA_DV
    tg = x_ref.shape[0]

    @pl.when(pl.program_id(1) == 0)
    def _():
        s_ref[...] = s0_ref[0]

    lane_grp = lax.broadcasted_iota(jnp.int32, (c, 128), 1) // dk

    def tile_head(x, h):
        v = x
        for s in range(1, GLA_HEADS):
            v = jnp.where(lane_grp == (h + s) % GLA_HEADS, pltpu.roll(x, s * dk, 1), v)
        return jnp.concatenate([v] * (c * dk // 128), axis=1)

    def body(ci, _):
        r0 = pl.multiple_of(ci * c, c)
        q = x_ref[pl.ds(r0, c), 0:128] * dk ** -0.5
        k = x_ref[pl.ds(r0, c), 128:256]
        la = la_ref[pl.ds(r0, c), :]
        b = jnp.dot(tril_ref[...], la, precision=HIGHEST, preferred_element_type=F32)
        bt = b.T
        kt = k.T
        b_last_t = bt[:, c - 1:c]
        kdt = kt * jnp.exp(b_last_t - bt)
        qd = q * jnp.exp(b)
        diag, causal = diag_ref[...], causal_ref[...]
        for h in range(GLA_HEADS):
            v = x_ref[pl.ds(r0, c), 256 + h * dv:256 + (h + 1) * dv]
            og = x_ref[pl.ds(r0, c), 512 + h * dv:512 + (h + 1) * dv]
            s_h = s_ref[h * dk:(h + 1) * dk, :]
            b_i, q_i, k_i = tile_head(b, h), tile_head(q, h), tile_head(k, h)
            b_j = jnp.sum(b_i * diag, axis=0, keepdims=True)
            k_j = jnp.sum(k_i * diag, axis=0, keepdims=True)
            p = q_i * k_j * jnp.exp(jnp.minimum(b_i - b_j, 0.0)) * causal
            att = jnp.dot(p.astype(BF16), gsum_ref[...], preferred_element_type=F32)
            o = jnp.dot(att.astype(BF16), v.astype(BF16), preferred_element_type=F32)
            o = o + jnp.dot(qd[:, h * dk:(h + 1) * dk].astype(BF16), s_h.astype(BF16), preferred_element_type=F32)
            s_ref[h * dk:(h + 1) * dk, :] = jnp.exp(b_last_t[h * dk:(h + 1) * dk, :]) * s_h + jnp.dot(
                kdt[h * dk:(h + 1) * dk, :].astype(BF16), v.astype(BF16), preferred_element_type=F32)
            o = o * lax.rsqrt(jnp.mean(o * o, axis=-1, keepdims=True) + RMS_EPS) * gn_ref[...]
            o_ref[pl.ds(r0, c), h * dv:(h + 1) * dv] = o * (og * jax.nn.sigmoid(og))
        return 0

    lax.fori_loop(0, tg // c, body, 0, unroll=min(4, tg // c))

    @pl.when(pl.program_id(1) == pl.num_programs(1) - 1)
    def _():
        sfin_ref[0] = s_ref[...]


def gla_block(gla, loga, s0, gnorm, b, chunk):
    m = gla.shape[0]
    t = m // b
    tg = _pick_tile(t, 256)
    nt = t // tg
    consts = _gla_consts(chunk)
    hk = GLA_HEADS * GLA_DK
    out, sfin = pl.pallas_call(
        functools.partial(_gla_kernel, chunk=chunk),
        grid=(b, nt),
        in_specs=[
            pl.BlockSpec((tg, 768), lambda i, j: (i * nt + j, 0)),
            pl.BlockSpec((tg, 128), lambda i, j: (i * nt + j, 0)),
            pl.BlockSpec((1, hk, GLA_DV), lambda i, j: (i, 0, 0)),
        ] + [pl.BlockSpec(a.shape, lambda i, j: (0, 0)) for a in consts]
        + [pl.BlockSpec((1, GLA_DV), lambda i, j: (0, 0))],
        out_specs=[
            pl.BlockSpec((tg, GROUP_WIDTH), lambda i, j: (i * nt + j, 0)),
            pl.BlockSpec((1, hk, GLA_DV), lambda i, j: (i, 0, 0)),
        ],
        out_shape=[jax.ShapeDtypeStruct((m, GROUP_WIDTH), F32), jax.ShapeDtypeStruct((b, hk, GLA_DV), F32)],
        scratch_shapes=[pltpu.VMEM((hk, GLA_DV), F32)],
        compiler_params=_cparams("parallel", "arbitrary"),
        name="gla_block",
    )(gla, loga, s0, *consts, gnorm.reshape(1, GLA_DV))
    return out, sfin


CAUSAL_STEP = 512
KEY_CHUNK = 2048
NEG_INF = float("-inf")
NT_DIMS = (((1,), (1,)), ((), ()))


def _dot_nt(a, b):
    return lax.dot_general(a, b, NT_DIMS, preferred_element_type=F32)


def _dot3(a, b):
    hi = a.astype(BF16)
    r1 = a - hi.astype(F32)
    mid = r1.astype(BF16)
    lo = (r1 - mid.astype(F32)).astype(BF16)
    return (jnp.dot(hi, b, preferred_element_type=F32) + jnp.dot(mid, b, preferred_element_type=F32)
            + jnp.dot(lo, b, preferred_element_type=F32))


def _softmax_parts(scores, masks):
    neg = [jnp.where(mk, s, NEG_INF) for s, mk in zip(scores, masks)]
    m = functools.reduce(jnp.maximum, [jnp.max(s, axis=-1, keepdims=True) for s in neg])
    m = jnp.where(m == NEG_INF, 0.0, m)
    es = [jnp.exp(s - m) for s in neg]
    tot = functools.reduce(jnp.add, [jnp.sum(e, axis=-1, keepdims=True) for e in es])
    return es, 1.0 / jnp.maximum(tot, 1e-30)


def _topk_select(score, jidx, k):
    big = jnp.float32(1e9)
    work, sel = score, jnp.zeros(score.shape, F32)
    for _ in range(k):
        m = jnp.max(work, axis=-1, keepdims=True)
        idx = jnp.min(jnp.where(work == m, jidx, big), axis=-1, keepdims=True)
        pick = jidx == idx
        sel = jnp.where(pick, 1.0, sel)
        work = jnp.where(pick, NEG_INF, work)
    return jnp.where(score > NEG_INF, sel, 0.0)


def _stack_heads(x, pad_to=None):
    parts = [x[:, h * HEAD_DIM:(h + 1) * HEAD_DIM] for h in range(x.shape[1] // HEAD_DIM)]
    out = jnp.concatenate(parts, axis=0)
    if pad_to is not None:
        out = jnp.concatenate([out, jnp.zeros((out.shape[0], pad_to - HEAD_DIM), out.dtype)], axis=1)
    return out


def _block_expand_matrix(block):
    return jnp.asarray(np.arange(KEY_CHUNK)[None, :] // block == np.arange(128)[:, None], dtype=BF16)


def _page_specs(n_pages, feats, layer):
    def idx(b, j, pt, *, k):
        return (layer, pt[b, j * n_pages + k], 0, 0)
    return [pl.BlockSpec((None, None, feats, PAGE_SIZE), functools.partial(idx, k=k)) for k in range(n_pages)]


def _nsa_prep_kernel(*refs, n_src, paged):
    if paged:
        refs = refs[1:]
    srcs, wc_ref, slc_ref, p_ref, xs_ref = refs[:n_src], refs[n_src], refs[n_src + 1], refs[n_src + 2], refs[n_src + 3]
    j = pl.program_id(1)
    rows = PAGE_SIZE if paged else srcs[0].shape[0]
    for k, src in enumerate(srcs):
        r0 = pl.multiple_of((j * n_src + k) * rows, rows)
        if paged:
            xs_ref[pl.ds(r0, rows), :] = src[0:128, :].T
            slc_ref[0, :, k * rows:(k + 1) * rows] = src[128:256, :].astype(BF16)
        else:
            xs_ref[pl.ds(r0, rows), :] = src[:, 0:128]
            slc_ref[0, k * rows:(k + 1) * rows, :] = src[:, 128:256].astype(BF16)

    @pl.when(j == pl.num_programs(1) - 1)
    def _():
        n = xs_ref.shape[0] // CMP_STRIDE
        acc = jnp.zeros((n, 512), F32)
        for r in range(CMP_STRIDE):
            x = xs_ref[pl.ds(r, n, stride=CMP_STRIDE), :].astype(BF16)
            acc = acc + jnp.dot(x, wc_ref[r], preferred_element_type=F32)
        p_ref[0] = acc


def _nsa_cmp_weights(w1):
    w = w1.reshape(2, 2, CMP_STRIDE, HEAD_DIM, 128)
    z = jnp.zeros((CMP_STRIDE, HEAD_DIM, 128), w1.dtype)
    top = jnp.concatenate([w[0, 0], z, w[0, 1], z], axis=-1)
    bot = jnp.concatenate([z, w[1, 0], z, w[1, 1]], axis=-1)
    return jnp.concatenate([top, bot], axis=1).astype(BF16)


def nsa_prep(wc, rows=None, cache=None, layer=None, page_table=None):
    if cache is None:
        b, t, _ = rows.shape
        kern = functools.partial(_nsa_prep_kernel, n_src=1, paged=False)
        return pl.pallas_call(
            kern, grid=(b, 1),
            in_specs=[pl.BlockSpec((None, t, 256), lambda i, j: (i, 0, 0)), pl.BlockSpec(wc.shape, lambda i, j: (0, 0, 0))],
            out_specs=[pl.BlockSpec((1, t, 128), lambda i, j: (i, 0, 0)),
                       pl.BlockSpec((1, t // CMP_STRIDE, 512), lambda i, j: (i, 0, 0))],
            out_shape=[jax.ShapeDtypeStruct((b, t, 128), BF16), jax.ShapeDtypeStruct((b, t // CMP_STRIDE, 512), F32)],
            scratch_shapes=[pltpu.VMEM((t, 128), F32)],
            compiler_params=_cparams("parallel", "arbitrary"), name="nsa_prep",
        )(rows, wc)
    b, n_pages = page_table.shape
    L = n_pages * PAGE_SIZE
    npg = 8
    kern = functools.partial(_nsa_prep_kernel, n_src=npg, paged=True)
    grid_spec = pltpu.PrefetchScalarGridSpec(
        num_scalar_prefetch=1, grid=(b, n_pages // npg),
        in_specs=_page_specs(npg, 256, layer) + [pl.BlockSpec(wc.shape, lambda i, j, pt: (0, 0, 0))],
        out_specs=[pl.BlockSpec((1, 128, npg * PAGE_SIZE), lambda i, j, pt: (i, 0, j)),
                   pl.BlockSpec((1, L // CMP_STRIDE, 512), lambda i, j, pt: (i, 0, 0))],
        scratch_shapes=[pltpu.VMEM((L, 128), F32)])
    return pl.pallas_call(
        kern, grid_spec=grid_spec,
        out_shape=[jax.ShapeDtypeStruct((b, 128, L), BF16), jax.ShapeDtypeStruct((b, L // CMP_STRIDE, 512), F32)],
        compiler_params=_cparams("parallel", "arbitrary"), name="nsa_prep_paged",
    )(page_table, *([cache] * npg), wc)


def _nsa_attend_kernel(*refs, tq, pos0, n_cmp, n_sel, n_chunks, has_new, t_total):
    (qraw_ref, qrot_ref, gate_ref, p_ref, cvec_ref, w2_ref, ov_ref, e0_ref, slc_ref, win_ref) = refs[:10]
    if has_new:
        newrow_ref, newwin_ref = refs[10:12]
        o_ref, kcvc_ref, oslc_ref = refs[12:]
    else:
        o_ref, kcvc_ref, oslc_ref = refs[10:]
    qt = pl.program_id(1)
    scale = HEAD_DIM ** -0.5
    r = NSA_HEADS * tq
    ncp = p_ref.shape[1]

    @pl.when(qt == 0)
    def _():
        p = p_ref[0]
        hid = p[:, 0:256] + pltpu.roll(p[:, 256:512], ncp - 1, 0) + cvec_ref[...]
        kcvc_ref[...] = jnp.dot(jax.nn.gelu(hid).astype(BF16), w2_ref[...], preferred_element_type=F32).astype(BF16)

    q0 = qt * tq
    p_row = pos0 + q0 + lax.broadcasted_iota(jnp.int32, (r, 1), 0) % tq
    p_q = pos0 + q0 + lax.broadcasted_iota(jnp.int32, (tq, 1), 0)
    q_raw = _stack_heads(qraw_ref[...], 128).astype(BF16)
    q_rot = _stack_heads(qrot_ref[...], 128).astype(BF16)
    kcvc = kcvc_ref[...]

    n_idx = lax.broadcasted_iota(jnp.int32, (r, ncp), 1)
    cmask = (n_idx * CMP_STRIDE + (CMP_BLOCK - 1) <= p_row) & (n_idx < n_cmp)
    (ec,), inv = _softmax_parts([_dot_nt(q_raw, kcvc) * scale], [cmask])
    pc = ec * inv
    o_cmp = jnp.dot(pc.astype(BF16), kcvc, preferred_element_type=F32)[:, HEAD_DIM:]
    imp_h = _dot3(pc, ov_ref[...])
    imp = functools.reduce(jnp.add, [imp_h[h * tq:(h + 1) * tq] for h in range(NSA_HEADS)])

    nsl = imp.shape[1]
    lane = lax.broadcasted_iota(jnp.int32, (tq, nsl), 1)
    jblk = (lane // 128) * (KEY_CHUNK // SEL_BLOCK) + lane % 128
    lane_ok = (lane % 128 < KEY_CHUNK // SEL_BLOCK) & (jblk < n_sel)
    cur = p_q // SEL_BLOCK
    forced = (jblk == 0) | (jblk == cur)
    score = jnp.where(forced, FORCE_SCORE, jnp.where(jblk < cur, imp, NEG_INF))
    score = jnp.where(lane_ok, score, NEG_INF)
    jf = jnp.where(lane_ok, jblk, 1 << 20).astype(F32)
    sel = _topk_select(score, jf, min(SEL_TOPK, n_sel))
    sel_r = jnp.concatenate([sel] * NSA_HEADS, axis=0).astype(BF16)

    klen = e0_ref.shape[1]

    def selected(ext):
        scores, masks, pvs = [], [], []
        kl = lax.broadcasted_iota(jnp.int32, (r, ext), 1)
        for c in range(n_chunks):
            picked = jnp.dot(sel_r[:, c * 128:(c + 1) * 128], e0_ref[:, 0:ext], preferred_element_type=F32) > 0.5
            masks.append(picked & (kl + c * klen <= p_row))
            if has_new:
                kv = slc_ref[0, :, c * klen:c * klen + ext]
                scores.append(jnp.dot(q_rot, kv, preferred_element_type=F32) * scale)
                pvs.append(functools.partial(_dot_nt, b=kv))
            else:
                kv = slc_ref[0, c * klen:c * klen + ext, :]
                scores.append(_dot_nt(q_rot, kv) * scale)
                pvs.append(functools.partial(lambda e, b: jnp.dot(e, b, preferred_element_type=F32), b=kv))
        if has_new:
            kv = newrow_ref[:, 128:256].astype(BF16)
            scores.append(_dot_nt(q_rot, kv) * scale)
            masks.append(pos0 + lax.broadcasted_iota(jnp.int32, (r, kv.shape[0]), 1) <= p_row)
            pvs.append(functools.partial(lambda e, b: jnp.dot(e, b, preferred_element_type=F32), b=kv))
        es, inv = _softmax_parts(scores, masks)
        o = functools.reduce(jnp.add, [pv(e.astype(BF16)) for e, pv in zip(es, pvs)])
        return (o * inv)[:, HEAD_DIM:]

    if has_new or n_chunks > 1 or klen % CAUSAL_STEP or klen == CAUSAL_STEP:
        o_slc = selected(klen)
    else:
        need = (q0 + tq + CAUSAL_STEP - 1) // CAUSAL_STEP
        for e in range(1, klen // CAUSAL_STEP + 1):
            @pl.when(need == e)
            def _(e=e):
                oslc_ref[...] = selected(e * CAUSAL_STEP)
        o_slc = oslc_ref[...]

    if has_new:
        wl = win_ref.shape[1]
        segs = [(win_ref[0].astype(BF16), pos0 - wl), (newwin_ref[...].astype(BF16), pos0)]
    else:
        span = min(t_total, WINDOW + tq)
        start = pl.multiple_of(jnp.clip(q0 - WINDOW, 0, t_total - span), tq)
        segs = [(win_ref[0, pl.ds(start, span), :].astype(BF16), pos0 + start)]
    scores, masks = [], []
    for kv, kp0 in segs:
        diff = p_row - (kp0 + lax.broadcasted_iota(jnp.int32, (r, kv.shape[0]), 1))
        scores.append(_dot_nt(q_rot, kv) * scale)
        masks.append((diff >= 0) & (diff < WINDOW) & (p_row - diff >= 0))
    es, inv = _softmax_parts(scores, masks)
    o_win = functools.reduce(jnp.add, [jnp.dot(e.astype(BF16), kv, preferred_element_type=F32) for e, (kv, _) in zip(es, segs)])
    o_win = (o_win * inv)[:, HEAD_DIM:]

    g = gate_ref[...]
    outs = []
    for h in range(NSA_HEADS):
        rows = slice(h * tq, (h + 1) * tq)
        outs.append(g[:, h:h + 1] * o_cmp[rows] + g[:, 4 + h:5 + h] * o_slc[rows] + g[:, 8 + h:9 + h] * o_win[rows])
    o_ref[...] = jnp.concatenate(outs, axis=1)


def _nsa_overlap(ncp, n_cmp, n_sel, nsl):
    per = KEY_CHUNK // SEL_BLOCK
    ov = np.zeros((ncp, nsl), np.float32)
    cs = np.arange(n_cmp) * CMP_STRIDE
    js = np.arange(n_sel) * SEL_BLOCK
    hit = (cs[:, None] < js[None, :] + SEL_BLOCK) & (cs[:, None] + CMP_BLOCK > js[None, :])
    lanes = (np.arange(n_sel) // per) * 128 + np.arange(n_sel) % per
    ov[:n_cmp, lanes] = hit
    return jnp.asarray(ov, dtype=BF16)


def nsa_attend(nq_raw, nq_rot, gates, pproj, cvec, w2bd, slc, win, b, t, pos0, new_rows=None, new_win=None):
    has_new = new_rows is not None
    tq = _pick_tile(t, 128)
    nq = t // tq
    l_slc = slc.shape[2] if has_new else slc.shape[1]
    l_ctx = l_slc + (t if has_new else 0)
    assert l_slc % KEY_CHUNK == 0 or l_slc < KEY_CHUNK
    assert not has_new or (t <= SEL_BLOCK and l_slc % KEY_CHUNK == 0 and t < CMP_STRIDE)
    n_chunks = max(l_slc // KEY_CHUNK, 1)
    n_cmp = (l_ctx - CMP_BLOCK) // CMP_STRIDE + 1
    n_sel = -(-l_ctx // SEL_BLOCK)
    per = KEY_CHUNK // SEL_BLOCK
    nsl = -(-n_sel // per) * 128
    ncp = pproj.shape[1]
    ov = _nsa_overlap(ncp, n_cmp, n_sel, nsl)
    e0 = _block_expand_matrix(SEL_BLOCK)
    if l_slc < KEY_CHUNK:
        e0 = e0[:, :l_slc]
    kern = functools.partial(_nsa_attend_kernel, tq=tq, pos0=pos0, n_cmp=n_cmp, n_sel=n_sel, n_chunks=n_chunks,
                             has_new=has_new, t_total=t)
    tile = lambda w: pl.BlockSpec((tq, w), lambda i, j: (i * nq + j, 0))
    const = lambda a: pl.BlockSpec(a.shape, lambda i, j: (0,) * a.ndim)
    batch = lambda a: pl.BlockSpec((1,) + a.shape[1:], lambda i, j: (i,) + (0,) * (a.ndim - 1))
    in_specs = [tile(256), tile(256), tile(128), batch(pproj), const(cvec), const(w2bd), const(ov), const(e0),
                batch(slc), batch(win)]
    args = [nq_raw, nq_rot, gates, pproj, cvec, w2bd, ov, e0, slc, win]
    if has_new:
        in_specs += [tile(256), tile(128)]
        args += [new_rows, new_win]
    return pl.pallas_call(
        kern, grid=(b, nq), in_specs=in_specs, out_specs=tile(256),
        out_shape=jax.ShapeDtypeStruct((b * t, GROUP_WIDTH), F32),
        scratch_shapes=[pltpu.VMEM((ncp, 128), BF16), pltpu.VMEM((NSA_HEADS * tq, HEAD_DIM), F32)],
        compiler_params=_cparams("parallel", "arbitrary"), name="nsa_attend",
    )(*args)


MOBA_PER_CHUNK = KEY_CHUNK // MOBA_BLOCK


def _moba_prep_kernel(*refs, n_src, paged):
    if paged:
        refs = refs[1:]
    srcs, avg_ref, kv_ref, mean_ref = refs[:n_src], refs[n_src], refs[n_src + 1], refs[n_src + 2]
    if paged:
        for k, src in enumerate(srcs):
            kv_ref[0, :, k * PAGE_SIZE:(k + 1) * PAGE_SIZE] = src[...].astype(BF16)
        mean_ref[0] = jnp.dot(kv_ref[0, 0:256, :], avg_ref[...], preferred_element_type=F32)
    else:
        kv_ref[0] = srcs[0][...].astype(BF16)
        mean_ref[0] = jnp.dot(avg_ref[...], kv_ref[0, :, 0:256], preferred_element_type=F32)


def moba_prep(rows=None, cache=None, layer=None, page_table=None):
    if cache is None:
        b, t, _ = rows.shape
        klen = min(KEY_CHUNK, t)
        n_chunks, n_src = t // klen, 1
        in_specs = [pl.BlockSpec((None, klen, 512), lambda i, j: (i, j, 0))]
        args, grid_kw, nsp = (rows,), {}, 0
    else:
        b, n_pages = page_table.shape
        n_src = min(KEY_CHUNK // PAGE_SIZE, n_pages)
        klen = n_src * PAGE_SIZE
        n_chunks = n_pages // n_src
        in_specs = _page_specs(n_src, 512, layer)
        args, nsp = (page_table,) + (cache,) * n_src, 1
    avg = (_block_expand_matrix(MOBA_BLOCK)[:, :klen].astype(F32) / MOBA_BLOCK).astype(BF16)
    if nsp:
        avg = avg.T
        out_specs = [pl.BlockSpec((1, 512, klen), lambda i, j, pt: (i, 0, j)),
                     pl.BlockSpec((1, 256, 128), lambda i, j, pt: (i, 0, j))]
        out_shape = [jax.ShapeDtypeStruct((b, 512, n_chunks * klen), BF16),
                     jax.ShapeDtypeStruct((b, 256, n_chunks * 128), F32)]
        cmap = lambda i, j, pt: (0, 0)
    else:
        out_specs = [pl.BlockSpec((1, klen, 512), lambda i, j: (i, j, 0)),
                     pl.BlockSpec((1, 128, 256), lambda i, j: (i, j, 0))]
        out_shape = [jax.ShapeDtypeStruct((b, n_chunks * klen, 512), BF16),
                     jax.ShapeDtypeStruct((b, n_chunks * 128, 256), F32)]
        cmap = lambda i, j: (0, 0)
    grid_spec = pltpu.PrefetchScalarGridSpec(
        num_scalar_prefetch=nsp, grid=(b, n_chunks),
        in_specs=in_specs + [pl.BlockSpec(avg.shape, cmap)], out_specs=out_specs)
    return pl.pallas_call(
        functools.partial(_moba_prep_kernel, n_src=n_src, paged=bool(nsp)), grid_spec=grid_spec, out_shape=out_shape,
        compiler_params=_cparams("parallel", "arbitrary"), name="moba_prep",
    )(*args, avg)


def _moba_attend_kernel(*refs, tq, pos0, n_cols, n_chunks, has_new):
    mq_ref, mean_ref, e0_ref, kv_ref = refs[:4]
    if has_new:
        new_ref = refs[4]
        o_ref, sel_ref, m_ref, l_ref, acc_ref = refs[5:]
    else:
        o_ref, sel_ref, m_ref, l_ref, acc_ref = refs[4:]
    qt, c = pl.program_id(1), pl.program_id(2)
    scale = HEAD_DIM ** -0.5
    r2 = 2 * tq
    klen = e0_ref.shape[1]
    q = mq_ref[...]
    q0 = pos0 + qt * tq
    p_row = q0 + lax.broadcasted_iota(jnp.int32, (r2, 1), 0) % tq

    def gate_scores(qh, means):
        return jnp.dot(qh, means, preferred_element_type=F32) if has_new else _dot_nt(qh, means)

    @pl.when(c == 0)
    def _():
        means = mean_ref[0].astype(BF16)
        nsl = n_chunks * 128
        lane = lax.broadcasted_iota(jnp.int32, (tq, nsl), 1)
        jblk = (lane // 128) * MOBA_PER_CHUNK + lane % 128
        lane_ok = (lane % 128 < MOBA_PER_CHUNK) & (jblk < n_cols)
        own = (q0 + lax.broadcasted_iota(jnp.int32, (tq, 1), 0)) // MOBA_BLOCK
        jf = jnp.where(lane_ok, jblk, 1 << 20).astype(F32)
        head_of_lane = lax.broadcasted_iota(jnp.int32, q.shape, 1) // HEAD_DIM
        for h in range(MOBA_HEADS):
            qh = jnp.where(head_of_lane == h, q, 0.0).astype(BF16)
            score = jnp.where(lane_ok & (jblk < own), gate_scores(qh, means), NEG_INF)
            sel = _topk_select(score, jf, MOBA_TOPK)
            for cc in range(n_chunks):
                sel_ref[cc, h * tq:(h + 1) * tq, :] = sel[:, cc * 128:(cc + 1) * 128]
        m_ref[...] = jnp.full(m_ref.shape, NEG_INF, F32)
        l_ref[...] = jnp.zeros(l_ref.shape, F32)
        acc_ref[...] = jnp.zeros(acc_ref.shape, F32)

    lane128 = lax.broadcasted_iota(jnp.int32, (tq, 128), 1)

    def pair_q(pr):
        x = q[:, 128 * pr:128 * (pr + 1)]
        return jnp.concatenate([jnp.where(lane128 < HEAD_DIM, x, 0.0), jnp.where(lane128 >= HEAD_DIM, x, 0.0)],
                               axis=0).astype(BF16)

    def update(pr, s, mask, v, v_feature_major=False):
        m_old = m_ref[pr][:, 0:1]
        neg = jnp.where(mask, s, NEG_INF)
        m_new = jnp.maximum(m_old, jnp.max(neg, axis=-1, keepdims=True))
        m_safe = jnp.where(m_new == NEG_INF, 0.0, m_new)
        alpha = jnp.exp(m_old - m_safe)
        e = jnp.exp(neg - m_safe)
        pv = _dot_nt(e.astype(BF16), v) if v_feature_major else jnp.dot(e.astype(BF16), v, preferred_element_type=F32)
        l_ref[pr] = alpha * l_ref[pr] + jnp.sum(e, axis=-1, keepdims=True)
        acc_ref[pr] = alpha * acc_ref[pr] + pv
        m_ref[pr] = jnp.broadcast_to(m_new, (r2, 128))

    own_row = p_row // MOBA_BLOCK

    def attend(ext):
        kpos = c * klen + lax.broadcasted_iota(jnp.int32, (r2, ext), 1)
        for pr in range(2):
            picked = jnp.dot(sel_ref[c, 2 * pr * tq:(2 * pr + 2) * tq, :].astype(BF16), e0_ref[:, 0:ext],
                             preferred_element_type=F32) > 0.5
            mask = (picked | (kpos // MOBA_BLOCK == own_row)) & (kpos <= p_row)
            if has_new:
                s = jnp.dot(pair_q(pr), kv_ref[0, 128 * pr:128 * (pr + 1), 0:ext], preferred_element_type=F32) * scale
                update(pr, s, mask, kv_ref[0, 256 + 128 * pr:256 + 128 * (pr + 1), 0:ext], v_feature_major=True)
            else:
                s = _dot_nt(pair_q(pr), kv_ref[0, 0:ext, 128 * pr:128 * (pr + 1)]) * scale
                update(pr, s, mask, kv_ref[0, 0:ext, 256 + 128 * pr:256 + 128 * (pr + 1)])

    if has_new or n_chunks > 1 or klen % CAUSAL_STEP or klen == CAUSAL_STEP:
        attend(klen)
    else:
        need = (qt * tq + tq + CAUSAL_STEP - 1) // CAUSAL_STEP
        for e in range(1, klen // CAUSAL_STEP + 1):
            @pl.when(need == e)
            def _(e=e):
                attend(e * CAUSAL_STEP)

    @pl.when(c == n_chunks - 1)
    def _():
        outs = []
        for pr in range(2):
            if has_new:
                t_new = new_ref.shape[0]
                kp = pos0 + lax.broadcasted_iota(jnp.int32, (r2, t_new), 1)
                mask = (kp // MOBA_BLOCK == own_row) & (kp <= p_row)
                s = _dot_nt(pair_q(pr), new_ref[:, 128 * pr:128 * (pr + 1)].astype(BF16)) * scale
                update(pr, s, mask, new_ref[:, 256 + 128 * pr:256 + 128 * (pr + 1)].astype(BF16))
            o = acc_ref[pr] / jnp.maximum(l_ref[pr][:, 0:1], 1e-30)
            outs += [o[:tq, :HEAD_DIM], o[tq:, HEAD_DIM:]]
        o_ref[...] = jnp.concatenate(outs, axis=1)


def moba_attend(mq, means, kv, b, t, pos0, new_rows=None):
    has_new = new_rows is not None
    tq = _pick_tile(t, 128)
    nq = t // tq
    l_kv = kv.shape[2] if has_new else kv.shape[1]
    klen = min(KEY_CHUNK, l_kv)
    n_chunks = l_kv // klen
    n_full = (l_kv + (t if has_new else 0)) // MOBA_BLOCK
    assert n_full >= MOBA_TOPK and l_kv % klen == 0 and klen % MOBA_BLOCK == 0
    assert not has_new or t <= MOBA_BLOCK
    e0 = _block_expand_matrix(MOBA_BLOCK)[:, :klen]
    kern = functools.partial(_moba_attend_kernel, tq=tq, pos0=pos0, n_cols=n_full, n_chunks=n_chunks, has_new=has_new)
    tile = lambda w: pl.BlockSpec((tq, w), lambda i, j, c: (i * nq + j, 0))
    kv_spec = (pl.BlockSpec((1, 512, klen), lambda i, j, c: (i, 0, c)) if has_new
               else pl.BlockSpec((1, klen, 512), lambda i, j, c: (i, c, 0)))
    in_specs = [tile(256), pl.BlockSpec((1,) + means.shape[1:], lambda i, j, c: (i, 0, 0)),
                pl.BlockSpec(e0.shape, lambda i, j, c: (0, 0)), kv_spec]
    args = [mq, means, e0, kv]
    if has_new:
        in_specs.append(tile(512))
        args.append(new_rows)
    return pl.pallas_call(
        kern, grid=(b, nq, n_chunks), in_specs=in_specs, out_specs=tile(256),
        out_shape=jax.ShapeDtypeStruct((b * t, GROUP_WIDTH), F32),
        scratch_shapes=[pltpu.VMEM((n_chunks, MOBA_HEADS * tq, 128), F32), pltpu.VMEM((2, 2 * tq, 128), F32),
                        pltpu.VMEM((2, 2 * tq, 128), F32), pltpu.VMEM((2, 2 * tq, 128), F32)],
        compiler_params=_cparams("parallel", "arbitrary", "arbitrary"), name="moba_attend",
    )(*args)


def _layer(h, p_emb, b, t, pos0, lw, past, final_gain, final):
    prompt = past is None
    pos = pos0 + jnp.arange(t, dtype=jnp.int32)
    h = ffn_block(h, lw['norm_ffn1'], lw['ffn1_w_in'], lw['ffn1_w_out'])
    (gla, loga, nq_raw, nq_rot, nsa_rows, win_rows, gates, su, mq, moba_rows) = mixin_block(
        h, lw['norm_mix'], lw['wcat'], _rope_table(pos), lw['gla_gate_w'], lw['gla_gate_b'], t)

    s0 = jnp.zeros((b, GLA_HEADS * GLA_DK, GLA_DV), F32) if prompt else past['gla'].reshape(b, -1, GLA_DV)
    o_gla, gla_state = gla_block(gla, loga, s0, lw['gla_norm'], b, GLA_CHUNK if prompt else t)
    gla_state = gla_state.reshape(b, GLA_HEADS, GLA_DK, GLA_DV)

    rows3 = nsa_rows.reshape(b, t, 4 * HEAD_DIM)
    win3 = win_rows.reshape(b, t, 2 * HEAD_DIM)
    if prompt:
        slc, pproj = nsa_prep(lw['nsa_wc'], rows=rows3)
        o_nsa = nsa_attend(nq_raw, nq_rot, gates, pproj, lw['nsa_cvec'], lw['nsa_w2bd'], slc, win3, b, t, 0)
        new_win = win3[:, -min(WINDOW, t):]
    else:
        slc, pproj = nsa_prep(lw['nsa_wc'], cache=past['nsa_cache'], layer=past['layer'], page_table=past['page_table'])
        wb = past['win'].reshape(b, -1, 2 * HEAD_DIM)
        o_nsa = nsa_attend(nq_raw, nq_rot, gates, pproj, lw['nsa_cvec'], lw['nsa_w2bd'], slc, wb, b, t, pos0,
                           new_rows=nsa_rows, new_win=win_rows)
        new_win = jnp.concatenate([wb, win3], axis=1)[:, -wb.shape[1]:]

    if prompt:
        h0 = jnp.zeros((b, 1, 2 * S5_LANES), F32)
    else:
        h0 = jnp.concatenate([past['s5_re'].reshape(b, 1, -1), past['s5_im'].reshape(b, 1, -1)], axis=-1)
    o_s5, s5_re, s5_im = s5_block(su, h0, lw['s5_pw'], lw['s5_bcat'], lw['s5_ccat'], lw['s5_d'],
                                  lw['s5_glu_w'], lw['s5_glu_b'], b)

    mrows = moba_rows.reshape(b, t, 2, MOBA_HEADS, HEAD_DIM)
    if prompt:
        mkv, means = moba_prep(rows=moba_rows.reshape(b, t, 512))
        o_moba = moba_attend(mq, means, mkv, b, t, 0)
    else:
        mkv, means = moba_prep(cache=past['moba_cache'], layer=past['layer'], page_table=past['page_table'])
        o_moba = moba_attend(mq, means, mkv, b, t, pos0, new_rows=moba_rows)

    h = mixout_block(h, (o_gla, o_nsa, o_s5, o_moba), lw['w_mix_out'])
    h = ffn_block(h, lw['norm_ffn2'], lw['ffn2_w_in'], lw['ffn2_w_out'])
    h = ple_block(h, p_emb, lw['norm_ple'], lw['ple_w'], lw['ple_gate_w'], final_gain, final)
    state = {'nsa': nsa_rows.reshape(b, t, 4, HEAD_DIM), 'moba': mrows,
             'win': new_win.reshape(b, -1, 2, HEAD_DIM), 'gla': gla_state, 's5_re': s5_re, 's5_im': s5_im}
    return h, state


def kernel(x_prompt, x_sample, p_prompt, p_sample, cache_nsa_kv, cache_moba_kv, state_nsa_win, state_gla,
           state_s5_re, state_s5_im, page_table, norm_ffn1, ffn1_w_in, ffn1_w_out, norm_mix, w_mix_in, w_mix_out,
           gla_gate_w, gla_gate_b, gla_norm, nsa_cmp_pe, nsa_cmp_w1, nsa_cmp_w2, s5_a_re, s5_a_im, s5_b_re, s5_b_im,
           s5_c_re, s5_c_im, s5_d, s5_log_dt, s5_glu_w, s5_glu_b, norm_ffn2, ffn2_w_in, ffn2_w_out, norm_ple, ple_w,
           ple_gate_w, norm_final):
    bp, tp, d = x_prompt.shape
    bs, ts, _ = x_sample.shape
    depth = p_prompt.shape[0]
    past_len = page_table.shape[1] * PAGE_SIZE
    params = {'norm_ffn1': norm_ffn1, 'norm_mix': norm_mix, 'gla_gate_w': gla_gate_w, 'gla_gate_b': gla_gate_b,
              'gla_norm': gla_norm, 'nsa_cmp_pe': nsa_cmp_pe, 'nsa_cmp_w1': nsa_cmp_w1, 'nsa_cmp_w2': nsa_cmp_w2,
              's5_a_re': s5_a_re, 's5_a_im': s5_a_im, 's5_b_re': s5_b_re, 's5_b_im': s5_b_im, 's5_c_re': s5_c_re,
              's5_c_im': s5_c_im, 's5_d': s5_d, 's5_log_dt': s5_log_dt, 's5_glu_w': s5_glu_w, 's5_glu_b': s5_glu_b,
              'norm_ffn2': norm_ffn2, 'norm_ple': norm_ple}
    hp = x_prompt.reshape(bp * tp, d)
    hs = x_sample.reshape(bs * ts, d)
    nsa_cache = jnp.transpose(cache_nsa_kv, (0, 1, 3, 4, 2)).reshape(cache_nsa_kv.shape[:2] + (4 * HEAD_DIM, PAGE_SIZE))
    moba_cache = jnp.transpose(cache_moba_kv, (0, 1, 3, 4, 5, 2)).reshape(
        cache_moba_kv.shape[:2] + (2 * MOBA_HEADS * HEAD_DIM, PAGE_SIZE))
    st_p, st_s = [], []
    for i in range(depth):
        lw = {k: v[i] for k, v in params.items()}
        lw['wcat'] = _mixin_weights(w_mix_in[i])
        for name, w in (('ffn1_w_in', ffn1_w_in), ('ffn1_w_out', ffn1_w_out), ('ffn2_w_in', ffn2_w_in),
                        ('ffn2_w_out', ffn2_w_out), ('w_mix_out', w_mix_out), ('ple_w', ple_w),
                        ('ple_gate_w', ple_gate_w)):
            lw[name] = w[i].astype(BF16)
        lw['s5_pw'], lw['s5_bcat'], lw['s5_ccat'] = _s5_params(s5_a_re[i], s5_a_im[i], s5_b_re[i], s5_b_im[i],
                                                               s5_c_re[i], s5_c_im[i], s5_log_dt[i])
        lw['s5_glu_w'] = s5_glu_w[i].astype(BF16)
        lw['nsa_wc'] = _nsa_cmp_weights(nsa_cmp_w1[i])
        pe, w1, w2 = nsa_cmp_pe[i], nsa_cmp_w1[i], nsa_cmp_w2[i]
        lw['nsa_cvec'] = jnp.concatenate([jnp.dot(pe[s].reshape(1, -1), w1[s], precision=HIGHEST) for s in range(2)], axis=1)
        zw = jnp.zeros_like(w2[0])
        lw['nsa_w2bd'] = jnp.concatenate([jnp.concatenate([w2[0], zw], axis=1),
                                          jnp.concatenate([zw, w2[1]], axis=1)], axis=0).astype(BF16)
        final = i == depth - 1
        hp, sp = _layer(hp, p_prompt[i].reshape(bp * tp, -1), bp, tp, 0, lw, None, norm_final, final)
        past = {'nsa_cache': nsa_cache, 'moba_cache': moba_cache, 'layer': i, 'page_table': page_table,
                'win': state_nsa_win[i], 'gla': state_gla[i], 's5_re': state_s5_re[i], 's5_im': state_s5_im[i]}
        hs, ss = _layer(hs, p_sample[i].reshape(bs * ts, -1), bs, ts, past_len, lw, past, norm_final, final)
        st_p.append(sp)
        st_s.append(ss)

    def stack(states, name):
        return jnp.stack([s[name] for s in states], axis=0)

    return (hp.reshape(bp, tp, d), hs.reshape(bs, ts, d),
            stack(st_p, 'nsa'), stack(st_s, 'nsa'), stack(st_p, 'moba'), stack(st_s, 'moba'),
            stack(st_p, 'win'), stack(st_s, 'win'), stack(st_p, 'gla'), stack(st_s, 'gla'),
            stack(st_p, 's5_re'), stack(st_s, 's5_re'), stack(st_p, 's5_im'), stack(st_s, 's5_im'))
```

```python
import functools
import math

import numpy as np
import jax
import jax.numpy as jnp
from jax import lax
from jax.experimental import pallas as pl
from jax.experimental.pallas import tpu as pltpu

F32 = jnp.float32
BF16 = jnp.bfloat16

HEAD_DIM = 64
ROPE_THETA = 10000.0
RMS_EPS = 1e-6
PAGE_SIZE = 128
GLA_HEADS = 4
GLA_DK = 32
GLA_DV = 64
GLA_GATE_RANK = 16
GLA_TAU = 16.0
GLA_CHUNK = 32
NSA_HEADS = 4
CMP_BLOCK = 32
CMP_STRIDE = 16
SEL_BLOCK = 64
SEL_TOPK = 16
WINDOW = 512
FORCE_SCORE = 1e4
S5_GROUP_SIZE = 16
S5_GROUPS = 16
S5_STATE = 64
MOBA_HEADS = 4
MOBA_BLOCK = 256
MOBA_TOPK = 3
GROUP_WIDTH = 256

VMEM_LIMIT_BYTES = 48 * 1024 * 1024


def _cparams(*sem):
    return pltpu.CompilerParams(dimension_semantics=sem, vmem_limit_bytes=VMEM_LIMIT_BYTES)


def _rms(x, g):
    return x * lax.rsqrt(jnp.mean(x * x, axis=-1, keepdims=True) + RMS_EPS) * g


def _pick_tile(m, pref):
    t = min(m, pref)
    while m % t:
        t //= 2
    return t


def _ffn_kernel(x_ref, g_ref, wg_ref, wu_ref, wo_ref, o_ref, xn_ref, acc_ref):
    j = pl.program_id(1)

    @pl.when(j == 0)
    def _():
        xn_ref[...] = _rms(x_ref[...], g_ref[...]).astype(BF16)
        acc_ref[...] = jnp.zeros_like(acc_ref)

    xn = xn_ref[...]
    g = jnp.dot(xn, wg_ref[...], preferred_element_type=F32)
    u = jnp.dot(xn, wu_ref[...], preferred_element_type=F32)
    act = (g * jax.nn.sigmoid(g) * u).astype(BF16)
    acc_ref[...] += jnp.dot(act, wo_ref[...], preferred_element_type=F32)

    @pl.when(j == pl.num_programs(1) - 1)
    def _():
        o_ref[...] = x_ref[...] + 0.5 * acc_ref[...]


def ffn_block(x, gain, w_in, w_out):
    m, d = x.shape
    f = w_out.shape[0]
    tm = _pick_tile(m, 1024)
    tf = 256
    nf = f // tf
    return pl.pallas_call(
        _ffn_kernel,
        grid=(m // tm, nf),
        in_specs=[
            pl.BlockSpec((tm, d), lambda i, j: (i, 0)),
            pl.BlockSpec((1, d), lambda i, j: (0, 0)),
            pl.BlockSpec((d, tf), lambda i, j: (0, j)),
            pl.BlockSpec((d, tf), lambda i, j: (0, j + nf)),
            pl.BlockSpec((tf, d), lambda i, j: (j, 0)),
        ],
        out_specs=pl.BlockSpec((tm, d), lambda i, j: (i, 0)),
        out_shape=jax.ShapeDtypeStruct((m, d), F32),
        scratch_shapes=[pltpu.VMEM((tm, d), BF16), pltpu.VMEM((tm, d), F32)],
        compiler_params=_cparams("parallel", "arbitrary"),
        name="ffn_block",
    )(x, gain.reshape(1, d), w_in, w_in, w_out)


MIXIN_OUT_WIDTHS = (768, 128, 256, 256, 256, 128, 128, 256, 256, 512)


def _mixin_kernel(h_ref, g_ref, w_ref, tab_ref, gw_ref, gb_ref,
                  gla_o, loga_o, nqraw_o, nqrot_o, nsa_o, win_o, gate_o, su_o, mq_o, moba_o):
    xn = _rms(h_ref[...], g_ref[...]).astype(BF16)

    def mm(a, b):
        return jnp.dot(xn, w_ref[:, a:b], preferred_element_type=F32)

    c256, s256 = tab_ref[:, 0:256], tab_ref[:, 256:512]
    gla_o[...] = mm(0, 768)
    z = mm(768, 896)
    la = jnp.dot(z, gw_ref[...], preferred_element_type=F32) + gb_ref[...]
    loga_o[...] = (jnp.minimum(la, 0.0) - jnp.log(1.0 + jnp.exp(-jnp.abs(la)))) * (1.0 / GLA_TAU)
    nq = mm(896, 1152)
    nqraw_o[...] = nq
    nqrot_o[...] = nq * c256 + mm(1152, 1408) * s256
    nsa_o[...] = mm(1408, 1664) * tab_ref[:, 512:768] + mm(1792, 2048) * tab_ref[:, 768:1024]
    win_o[...] = mm(1664, 1792) * tab_ref[:, 1024:1152] + mm(2048, 2176) * tab_ref[:, 1152:1280]
    gate_o[...] = jax.nn.sigmoid(mm(2176, 2304))
    su_o[...] = mm(2304, 2560)
    mq_o[...] = mm(2560, 2816) * c256 + mm(2816, 3072) * s256
    moba_o[:, 0:256] = mm(3072, 3328) * c256 + mm(3328, 3584) * s256
    moba_o[:, 256:512] = mm(3584, 3840)


def _rot_cols(w):
    d, n = w.shape
    w = w.reshape(d, n // HEAD_DIM, 2, HEAD_DIM // 2)
    return jnp.stack([-w[:, :, 1], w[:, :, 0]], axis=2).reshape(d, n)


def _mixin_weights(w):
    d = w.shape[0]
    z = lambda n: jnp.zeros((d, n), w.dtype)
    gq, gk, gv, gz, gog = w[:, 0:128], w[:, 128:256], w[:, 256:512], w[:, 512:528], w[:, 528:784]
    nq, nkv, ng, su = w[:, 784:1040], w[:, 1040:1424], w[:, 1424:1436], w[:, 1436:1692]
    mq, mk, mv = w[:, 1692:1948], w[:, 1948:2204], w[:, 2204:2460]
    r_nsa = jnp.concatenate([z(128), _rot_cols(nkv[:, 128:192]), z(64)], axis=1)
    r_win = jnp.concatenate([_rot_cols(nkv[:, 256:320]), z(64)], axis=1)
    cat = jnp.concatenate([gq, gk, gv, gog, gz, z(112), nq, _rot_cols(nq), nkv, r_nsa, r_win, ng, z(116), su,
                           mq, _rot_cols(mq), mk, _rot_cols(mk), mv], axis=1)
    return cat.astype(BF16)


def _rope_table(pos):
    half = HEAD_DIM // 2
    inv = ROPE_THETA ** (-jnp.arange(half, dtype=F32) / half)
    ang = pos.astype(F32)[:, None] * inv
    c, s = jnp.cos(ang), jnp.sin(ang)
    c64, s64 = jnp.concatenate([c, c], -1), jnp.concatenate([s, s], -1)
    one, zero = jnp.ones_like(c64), jnp.zeros_like(c64)
    return jnp.concatenate([c64, c64, c64, c64, s64, s64, s64, s64, one, one, c64, one, zero, zero, s64, zero,
                            c64, one, s64, zero], axis=-1)


def mixin_block(h, gain, wcat, table, gate_w, gate_b, t_len):
    m, d = h.shape
    tm = _pick_tile(m, 512)
    if t_len % tm == 0:
        ntab = t_len // tm
        tab_map = lambda i: (i % ntab, 0)
    else:
        table = jnp.tile(table, (tm // t_len, 1))
        tab_map = lambda i: (0, 0)
    gw = jnp.zeros((128, 128), F32).at[:GLA_GATE_RANK].set(gate_w)
    return pl.pallas_call(
        _mixin_kernel,
        grid=(m // tm,),
        in_specs=[
            pl.BlockSpec((tm, d), lambda i: (i, 0)),
            pl.BlockSpec((1, d), lambda i: (0, 0)),
            pl.BlockSpec(wcat.shape, lambda i: (0, 0)),
            pl.BlockSpec((tm, table.shape[1]), tab_map),
            pl.BlockSpec((128, 128), lambda i: (0, 0)),
            pl.BlockSpec((1, 128), lambda i: (0, 0)),
        ],
        out_specs=[pl.BlockSpec((tm, n), lambda i: (i, 0)) for n in MIXIN_OUT_WIDTHS],
        out_shape=[jax.ShapeDtypeStruct((m, n), F32) for n in MIXIN_OUT_WIDTHS],
        compiler_params=_cparams("parallel"),
        name="mixin_block",
    )(h, gain.reshape(1, d), wcat, table, gw, gate_b.reshape(1, 128))


def _mixout_kernel(h_ref, a_ref, b_ref, c_ref, d_ref, w_ref, o_ref):
    acc = h_ref[...]
    for k, r in enumerate((a_ref, b_ref, c_ref, d_ref)):
        acc = acc + jnp.dot(r[...].astype(BF16), w_ref[k * GROUP_WIDTH:(k + 1) * GROUP_WIDTH, :],
                            preferred_element_type=F32)
    o_ref[...] = acc


def mixout_block(h, parts, w):
    m, d = h.shape
    tm = _pick_tile(m, 512)
    return pl.pallas_call(
        _mixout_kernel,
        grid=(m // tm,),
        in_specs=[pl.BlockSpec((tm, d), lambda i: (i, 0))]
        + [pl.BlockSpec((tm, GROUP_WIDTH), lambda i: (i, 0)) for _ in range(4)]
        + [pl.BlockSpec((d, d), lambda i: (0, 0))],
        out_specs=pl.BlockSpec((tm, d), lambda i: (i, 0)),
        out_shape=jax.ShapeDtypeStruct((m, d), F32),
        compiler_params=_cparams("parallel"),
        name="mixout_block",
    )(h, *parts, w)


def _ple_kernel(h_ref, p_ref, g_ref, wp_ref, wg_ref, gf_ref, o_ref, *, final):
    h = h_ref[...]
    gate = jax.nn.sigmoid(jnp.dot(_rms(h, g_ref[...]).astype(BF16), wg_ref[...], preferred_element_type=F32))
    e = jnp.dot(p_ref[...].astype(BF16), wp_ref[...], preferred_element_type=F32)
    y = h + e * gate
    o_ref[...] = _rms(y, gf_ref[...]) if final else y


def ple_block(h, p, gain, ple_w, gate_w, final_gain, final):
    m, d = h.shape
    pd = p.shape[1]
    tm = _pick_tile(m, 512)
    return pl.pallas_call(
        functools.partial(_ple_kernel, final=final),
        grid=(m // tm,),
        in_specs=[
            pl.BlockSpec((tm, d), lambda i: (i, 0)),
            pl.BlockSpec((tm, pd), lambda i: (i, 0)),
            pl.BlockSpec((1, d), lambda i: (0, 0)),
            pl.BlockSpec((pd, d), lambda i: (0, 0)),
            pl.BlockSpec((d, d), lambda i: (0, 0)),
            pl.BlockSpec((1, d), lambda i: (0, 0)),
        ],
        out_specs=pl.BlockSpec((tm, d), lambda i: (i, 0)),
        out_shape=jax.ShapeDtypeStruct((m, d), F32),
        compiler_params=_cparams("parallel"),
        name="ple_block",
    )(h, p, gain.reshape(1, d), ple_w, gate_w, final_gain.reshape(1, d))


S5_LANES = S5_GROUPS * S5_STATE
S5_TILE = 8
HIGHEST = lax.Precision.HIGHEST


def _s5_kernel(u_ref, h0_ref, pw_ref, bcat_ref, ccat_ref, d_ref, gw_ref, gb_ref,
               o_ref, hre_ref, him_ref, bu_ref, h_ref, carry_ref):
    n = S5_LANES
    tb = u_ref.shape[0]

    @pl.when(pl.program_id(1) == 0)
    def _():
        carry_ref[...] = jnp.broadcast_to(h0_ref[0], (S5_TILE, 2 * n))

    u = u_ref[...]
    bu_ref[...] = jnp.dot(u.astype(BF16), bcat_ref[...], preferred_element_type=F32)

    def tile(k, carry):
        cr, ci = carry
        r0 = pl.multiple_of(k * S5_TILE, S5_TILE)
        xr = bu_ref[pl.ds(r0, S5_TILE), 0:n]
        xi = bu_ref[pl.ds(r0, S5_TILE), n:2 * n]
        for lvl, sh in enumerate((1, 2, 4)):
            pr = pw_ref[:, (2 * lvl) * n:(2 * lvl + 1) * n]
            pi = pw_ref[:, (2 * lvl + 1) * n:(2 * lvl + 2) * n]
            rr = pltpu.roll(xr, sh, 0)
            ri = pltpu.roll(xi, sh, 0)
            xr, xi = xr + pr * rr - pi * ri, xi + pr * ri + pi * rr
        pr, pi = pw_ref[:, 6 * n:7 * n], pw_ref[:, 7 * n:8 * n]
        xr, xi = xr + pr * cr - pi * ci, xi + pr * ci + pi * cr
        h_ref[pl.ds(r0, S5_TILE), 0:n] = xr
        h_ref[pl.ds(r0, S5_TILE), n:2 * n] = xi
        last_r = jnp.broadcast_to(xr[S5_TILE - 1:S5_TILE, :], (S5_TILE, n))
        last_i = jnp.broadcast_to(xi[S5_TILE - 1:S5_TILE, :], (S5_TILE, n))
        return last_r, last_i

    cr, ci = lax.fori_loop(0, tb // S5_TILE, tile, (carry_ref[:, 0:n], carry_ref[:, n:2 * n]),
                           unroll=min(4, tb // S5_TILE))
    carry_ref[:, 0:n] = cr
    carry_ref[:, n:2 * n] = ci
    hre_ref[0] = cr[0:1, :]
    him_ref[0] = ci[0:1, :]

    y = jnp.dot(h_ref[...].astype(BF16), ccat_ref[...], preferred_element_type=F32) + d_ref[...] * u
    g = jax.nn.gelu(y)
    gate = jnp.dot(g.astype(BF16), gw_ref[...], preferred_element_type=F32) + gb_ref[...]
    o_ref[...] = g * jax.nn.sigmoid(gate)


def _s5_params(a_re, a_im, b_re, b_im, c_re, c_im, log_dt):
    g, n = a_re.shape
    c = b_re.shape[-1]
    dt = jnp.exp(log_dt)[:, None]
    mag = jnp.exp(a_re * dt)
    abar_re, abar_im = mag * jnp.cos(a_im * dt), mag * jnp.sin(a_im * dt)
    den = a_re * a_re + a_im * a_im
    fr = ((abar_re - 1.0) * a_re + abar_im * a_im) / den
    fi = (abar_im * a_re - (abar_re - 1.0) * a_im) / den
    bbar_re = fr[..., None] * b_re - fi[..., None] * b_im
    bbar_im = fr[..., None] * b_im + fi[..., None] * b_re
    eye = jnp.eye(g, dtype=F32)
    blk_in = lambda m: jnp.einsum('gnc,gh->gchn', m, eye).reshape(g * c, g * n)
    blk_out = lambda m: jnp.einsum('gcn,gh->gnhc', m, eye).reshape(g * n, g * c)
    bcat = jnp.concatenate([blk_in(bbar_re), blk_in(bbar_im)], axis=1).astype(BF16)
    ccat = jnp.concatenate([blk_out(c_re), -blk_out(c_im)], axis=0).astype(BF16)
    ar, ai = abar_re.reshape(1, g * n), abar_im.reshape(1, g * n)
    pows = [(ar, ai)]
    for _ in range(S5_TILE - 1):
        pr, pi = pows[-1]
        pows.append((pr * ar - pi * ai, pr * ai + pi * ar))
    t = jnp.arange(S5_TILE)[:, None]
    cols = []
    for sh in (1, 2, 4):
        pr, pi = pows[sh - 1]
        cols += [jnp.where(t >= sh, pr, 0.0), jnp.where(t >= sh, pi, 0.0)]
    cols += [jnp.concatenate([p[0] for p in pows], axis=0), jnp.concatenate([p[1] for p in pows], axis=0)]
    return jnp.concatenate(cols, axis=1), bcat, ccat


def s5_block(su, h0, pw, bcat, ccat, dvec, glu_w, glu_b, b):
    m = su.shape[0]
    t = m // b
    tb = _pick_tile(t, 256)
    nt = t // tb
    n = S5_LANES
    out, hre, him = pl.pallas_call(
        _s5_kernel,
        grid=(b, nt),
        in_specs=[
            pl.BlockSpec((tb, GROUP_WIDTH), lambda i, j: (i * nt + j, 0)),
            pl.BlockSpec((1, 1, 2 * n), lambda i, j: (i, 0, 0)),
            pl.BlockSpec(pw.shape, lambda i, j: (0, 0)),
            pl.BlockSpec(bcat.shape, lambda i, j: (0, 0)),
            pl.BlockSpec(ccat.shape, lambda i, j: (0, 0)),
            pl.BlockSpec((1, GROUP_WIDTH), lambda i, j: (0, 0)),
            pl.BlockSpec((GROUP_WIDTH, GROUP_WIDTH), lambda i, j: (0, 0)),
            pl.BlockSpec((1, GROUP_WIDTH), lambda i, j: (0, 0)),
        ],
        out_specs=[
            pl.BlockSpec((tb, GROUP_WIDTH), lambda i, j: (i * nt + j, 0)),
            pl.BlockSpec((1, 1, n), lambda i, j: (i, 0, 0)),
            pl.BlockSpec((1, 1, n), lambda i, j: (i, 0, 0)),
        ],
        out_shape=[jax.ShapeDtypeStruct((m, GROUP_WIDTH), F32),
                   jax.ShapeDtypeStruct((b, 1, n), F32), jax.ShapeDtypeStruct((b, 1, n), F32)],
        scratch_shapes=[pltpu.VMEM((tb, 2 * n), F32), pltpu.VMEM((tb, 2 * n), F32), pltpu.VMEM((S5_TILE, 2 * n), F32)],
        compiler_params=_cparams("parallel", "arbitrary"),
        name="s5_block",
    )(su, h0, pw, bcat, ccat, dvec.reshape(1, -1), glu_w, glu_b.reshape(1, -1))
    return out, hre.reshape(b, S5_GROUPS, S5_STATE), him.reshape(b, S5_GROUPS, S5_STATE)


def _gla_consts(c):
    dk = GLA_DK
    tril = np.tril(np.ones((c, c), np.float32))
    lane_j = np.arange(c * dk) // dk
    diag = (np.arange(c)[:, None] == lane_j[None, :]).astype(np.float32)
    causal = (lane_j[None, :] <= np.arange(c)[:, None]).astype(np.float32)
    gsum = (lane_j[:, None] == np.arange(c)[None, :]).astype(np.float32)
    return jnp.asarray(tril), jnp.asarray(diag), jnp.asarray(causal), jnp.asarray(gsum, dtype=BF16)


def _gla_kernel(x_ref, la_ref, s0_ref, tril_ref, diag_ref, causal_ref, gsum_ref, gn_ref,
                o_ref, sfin_ref, s_ref, *, chunk):
    c, dk, dv = chunk, GLA_DK, GLA_DV
    tg = x_ref.shape[0]

    @pl.when(pl.program_id(1) == 0)
    def _():
        s_ref[...] = s0_ref[0]

    lane_grp = lax.broadcasted_iota(jnp.int32, (c, 128), 1) // dk

    def tile_head(x, h):
        v = x
        for s in range(1, GLA_HEADS):
            v = jnp.where(lane_grp == (h + s) % GLA_HEADS, pltpu.roll(x, s * dk, 1), v)
        return jnp.concatenate([v] * (c * dk // 128), axis=1)

    def body(ci, _):
        r0 = pl.multiple_of(ci * c, c)
        q = x_ref[pl.ds(r0, c), 0:128] * dk ** -0.5
        k = x_ref[pl.ds(r0, c), 128:256]
        la = la_ref[pl.ds(r0, c), :]
        b = jnp.dot(tril_ref[...], la, precision=HIGHEST, preferred_element_type=F32)
        bt = b.T
        kt = k.T
        b_last_t = bt[:, c - 1:c]
        kdt = kt * jnp.exp(b_last_t - bt)
        qd = q * jnp.exp(b)
        diag, causal = diag_ref[...], causal_ref[...]
        for h in range(GLA_HEADS):
            v = x_ref[pl.ds(r0, c), 256 + h * dv:256 + (h + 1) * dv]
            og = x_ref[pl.ds(r0, c), 512 + h * dv:512 + (h + 1) * dv]
            s_h = s_ref[h * dk:(h + 1) * dk, :]
            b_i, q_i, k_i = tile_head(b, h), tile_head(q, h), tile_head(k, h)
            b_j = jnp.sum(b_i * diag, axis=0, keepdims=True)
            k_j = jnp.sum(k_i * diag, axis=0, keepdims=True)
            p = q_i * k_j * jnp.exp(jnp.minimum(b_i - b_j, 0.0)) * causal
            att = jnp.dot(p.astype(BF16), gsum_ref[...], preferred_element_type=F32)
            o = jnp.dot(att.astype(BF16), v.astype(BF16), preferred_element_type=F32)
            o = o + jnp.dot(qd[:, h * dk:(h + 1) * dk].astype(BF16), s_h.astype(BF16), preferred_element_type=F32)
            s_ref[h * dk:(h + 1) * dk, :] = jnp.exp(b_last_t[h * dk:(h + 1) * dk, :]) * s_h + jnp.dot(
                kdt[h * dk:(h + 1) * dk, :].astype(BF16), v.astype(BF16), preferred_element_type=F32)
            o = o * lax.rsqrt(jnp.mean(o * o, axis=-1, keepdims=True) + RMS_EPS) * gn_ref[...]
            o_ref[pl.ds(r0, c), h * dv:(h + 1) * dv] = o * (og * jax.nn.sigmoid(og))
        return 0

    lax.fori_loop(0, tg // c, body, 0, unroll=min(4, tg // c))

    @pl.when(pl.program_id(1) == pl.num_programs(1) - 1)
    def _():
        sfin_ref[0] = s_ref[...]


def gla_block(gla, loga, s0, gnorm, b, chunk):
    m = gla.shape[0]
    t = m // b
    tg = _pick_tile(t, 256)
    nt = t // tg
    consts = _gla_consts(chunk)
    hk = GLA_HEADS * GLA_DK
    out, sfin = pl.pallas_call(
        functools.partial(_gla_kernel, chunk=chunk),
        grid=(b, nt),
        in_specs=[
            pl.BlockSpec((tg, 768), lambda i, j: (i * nt + j, 0)),
            pl.BlockSpec((tg, 128), lambda i, j: (i * nt + j, 0)),
            pl.BlockSpec((1, hk, GLA_DV), lambda i, j: (i, 0, 0)),
        ] + [pl.BlockSpec(a.shape, lambda i, j: (0, 0)) for a in consts]
        + [pl.BlockSpec((1, GLA_DV), lambda i, j: (0, 0))],
        out_specs=[
            pl.BlockSpec((tg, GROUP_WIDTH), lambda i, j: (i * nt + j, 0)),
            pl.BlockSpec((1, hk, GLA_DV), lambda i, j: (i, 0, 0)),
        ],
        out_shape=[jax.ShapeDtypeStruct((m, GROUP_WIDTH), F32), jax.ShapeDtypeStruct((b, hk, GLA_DV), F32)],
        scratch_shapes=[pltpu.VMEM((hk, GLA_DV), F32)],
        compiler_params=_cparams("parallel", "arbitrary"),
        name="gla_block",
    )(gla, loga, s0, *consts, gnorm.reshape(1, GLA_DV))
    return out, sfin


CAUSAL_STEP = 512
KEY_CHUNK = 2048
NEG_INF = float("-inf")
NT_DIMS = (((1,), (1,)), ((), ()))


def _dot_nt(a, b):
    return lax.dot_general(a, b, NT_DIMS, preferred_element_type=F32)


def _dot3(a, b):
    hi = a.astype(BF16)
    r1 = a - hi.astype(F32)
    mid = r1.astype(BF16)
    lo = (r1 - mid.astype(F32)).astype(BF16)
    return (jnp.dot(hi, b, preferred_element_type=F32) + jnp.dot(mid, b, preferred_element_type=F32)
            + jnp.dot(lo, b, preferred_element_type=F32))


MASK_BIAS = -(2.0 ** 100)


def _softmax_parts(neg):
    m = functools.reduce(jnp.maximum, [jnp.max(s, axis=-1, keepdims=True) for s in neg])
    m = jnp.where(m == NEG_INF, 0.0, m)
    es = [jnp.exp(s - m) for s in neg]
    tot = functools.reduce(jnp.add, [jnp.sum(e, axis=-1, keepdims=True) for e in es])
    return es, 1.0 / jnp.maximum(tot, 1e-30)


def _topk_select(score, jidx, k, lanes):
    if len(lanes) <= 4 * k:
        lane = lax.broadcasted_iota(jnp.int32, score.shape, 1)
        cnt = jnp.zeros(score.shape, F32)
        for li in lanes:
            col = score[:, li:li + 1]
            ge, gt = jnp.where(col >= score, 1.0, 0.0), jnp.where(col > score, 1.0, 0.0)
            cnt = cnt + jnp.where(lane > li, ge, gt)
        return jnp.where(score > NEG_INF, jnp.where(cnt < k, 1.0, 0.0), 0.0)
    big = jnp.float32(1e9)
    work, sel = score, jnp.zeros(score.shape, F32)
    for _ in range(k):
        m = jnp.max(work, axis=-1, keepdims=True)
        idx = jnp.min(jnp.where(work == m, jidx, big), axis=-1, keepdims=True)
        pick = jidx == idx
        sel = jnp.where(pick, 1.0, sel)
        work = jnp.where(pick, NEG_INF, work)
    return jnp.where(score > NEG_INF, sel, 0.0)


def _block_lanes(n_blocks, per_chunk):
    return [(j // per_chunk) * 128 + j % per_chunk for j in range(n_blocks)]


def _stack_heads(x, pad_to=None):
    parts = [x[:, h * HEAD_DIM:(h + 1) * HEAD_DIM] for h in range(x.shape[1] // HEAD_DIM)]
    out = jnp.concatenate(parts, axis=0)
    if pad_to is not None:
        out = jnp.concatenate([out, jnp.zeros((out.shape[0], pad_to - HEAD_DIM), out.dtype)], axis=1)
    return out


def _block_expand_matrix(block):
    return jnp.asarray(np.arange(KEY_CHUNK)[None, :] // block == np.arange(128)[:, None], dtype=BF16)


def _page_specs(n_pages, feats, layer):
    def idx(b, j, pt, *, k):
        return (layer, pt[b, j * n_pages + k], 0, 0)
    return [pl.BlockSpec((None, None, feats, PAGE_SIZE), functools.partial(idx, k=k)) for k in range(n_pages)]


def _nsa_prep_kernel(*refs, n_src, paged):
    if paged:
        refs = refs[1:]
    srcs, wc_ref, slc_ref, p_ref, xs_ref = refs[:n_src], refs[n_src], refs[n_src + 1], refs[n_src + 2], refs[n_src + 3]
    j = pl.program_id(1)
    rows = PAGE_SIZE if paged else srcs[0].shape[0]
    if paged:
        eye = (lax.broadcasted_iota(jnp.int32, (rows, rows), 0) == lax.broadcasted_iota(jnp.int32, (rows, rows), 1))
        eye = jnp.where(eye, 1.0, 0.0).astype(BF16)
    for k, src in enumerate(srcs):
        r0 = pl.multiple_of((j * n_src + k) * rows, rows)
        if paged:
            xs_ref[pl.ds(r0, rows), :] = _dot_nt(eye, src[0:128, :].astype(BF16))
            slc_ref[0, :, k * rows:(k + 1) * rows] = src[128:256, :].astype(BF16)
        else:
            xs_ref[pl.ds(r0, rows), :] = src[:, 0:128]
            slc_ref[0, k * rows:(k + 1) * rows, :] = src[:, 128:256].astype(BF16)

    @pl.when(j == pl.num_programs(1) - 1)
    def _():
        n = xs_ref.shape[0] // CMP_STRIDE
        acc = jnp.zeros((n, 512), F32)
        for r in range(CMP_STRIDE):
            x = xs_ref[pl.ds(r, n, stride=CMP_STRIDE), :].astype(BF16)
            acc = acc + jnp.dot(x, wc_ref[r], preferred_element_type=F32)
        p_ref[0] = acc


def _nsa_cmp_weights(w1):
    w = w1.reshape(2, 2, CMP_STRIDE, HEAD_DIM, 128)
    z = jnp.zeros((CMP_STRIDE, HEAD_DIM, 128), w1.dtype)
    top = jnp.concatenate([w[0, 0], z, w[0, 1], z], axis=-1)
    bot = jnp.concatenate([z, w[1, 0], z, w[1, 1]], axis=-1)
    return jnp.concatenate([top, bot], axis=1).astype(BF16)


def nsa_prep(wc, rows=None, cache=None, layer=None, page_table=None):
    if cache is None:
        b, t, _ = rows.shape
        kern = functools.partial(_nsa_prep_kernel, n_src=1, paged=False)
        return pl.pallas_call(
            kern, grid=(b, 1),
            in_specs=[pl.BlockSpec((None, t, 256), lambda i, j: (i, 0, 0)), pl.BlockSpec(wc.shape, lambda i, j: (0, 0, 0))],
            out_specs=[pl.BlockSpec((1, t, 128), lambda i, j: (i, 0, 0)),
                       pl.BlockSpec((1, t // CMP_STRIDE, 512), lambda i, j: (i, 0, 0))],
            out_shape=[jax.ShapeDtypeStruct((b, t, 128), BF16), jax.ShapeDtypeStruct((b, t // CMP_STRIDE, 512), F32)],
            scratch_shapes=[pltpu.VMEM((t, 128), F32)],
            compiler_params=_cparams("parallel", "arbitrary"), name="nsa_prep",
        )(rows, wc)
    b, n_pages = page_table.shape
    L = n_pages * PAGE_SIZE
    npg = min(32, n_pages)
    assert n_pages % npg == 0
    kern = functools.partial(_nsa_prep_kernel, n_src=npg, paged=True)
    grid_spec = pltpu.PrefetchScalarGridSpec(
        num_scalar_prefetch=1, grid=(b, n_pages // npg),
        in_specs=_page_specs(npg, 256, layer) + [pl.BlockSpec(wc.shape, lambda i, j, pt: (0, 0, 0))],
        out_specs=[pl.BlockSpec((1, 128, npg * PAGE_SIZE), lambda i, j, pt: (i, 0, j)),
                   pl.BlockSpec((1, L // CMP_STRIDE, 512), lambda i, j, pt: (i, 0, 0))],
        scratch_shapes=[pltpu.VMEM((L, 128), F32)])
    return pl.pallas_call(
        kern, grid_spec=grid_spec,
        out_shape=[jax.ShapeDtypeStruct((b, 128, L), BF16), jax.ShapeDtypeStruct((b, L // CMP_STRIDE, 512), F32)],
        compiler_params=_cparams("parallel", "arbitrary"), name="nsa_prep_paged",
    )(page_table, *([cache] * npg), wc)


def _nsa_attend_kernel(*refs, tq, pos0, n_cmp, n_sel, n_chunks, has_new, t_total):
    (qraw_ref, qrot_ref, gate_ref, p_ref, cvec_ref, w2_ref, ov_ref, e0_ref, slc_ref, win_ref) = refs[:10]
    if has_new:
        newrow_ref, newwin_ref = refs[10:12]
        o_ref, kcvc_ref, oslc_ref = refs[12:]
    else:
        o_ref, kcvc_ref, oslc_ref = refs[10:]
    qt = pl.program_id(1)
    scale = HEAD_DIM ** -0.5
    r = NSA_HEADS * tq
    ncp = p_ref.shape[1]

    @pl.when(qt == 0)
    def _():
        p = p_ref[0]
        hid = p[:, 0:256] + pltpu.roll(p[:, 256:512], ncp - 1, 0) + cvec_ref[...]
        kcvc_ref[...] = jnp.dot(jax.nn.gelu(hid).astype(BF16), w2_ref[...], preferred_element_type=F32).astype(BF16)

    q0 = qt * tq
    p_row = pos0 + q0 + lax.broadcasted_iota(jnp.int32, (r, 1), 0) % tq
    p_q = pos0 + q0 + lax.broadcasted_iota(jnp.int32, (tq, 1), 0)
    q_raw = (_stack_heads(qraw_ref[...], 128) * scale).astype(BF16)
    q_rot = (_stack_heads(qrot_ref[...], 128) * scale).astype(BF16)
    kcvc = kcvc_ref[...]

    n_idx = lax.broadcasted_iota(jnp.int32, (r, ncp), 1)
    last_visible = jnp.minimum(p_row - (CMP_BLOCK - 1), (n_cmp - 1) * CMP_STRIDE)
    (ec,), inv = _softmax_parts([jnp.where(n_idx * CMP_STRIDE <= last_visible, _dot_nt(q_raw, kcvc), NEG_INF)])
    pc = ec * inv
    o_cmp = jnp.dot(pc.astype(BF16), kcvc, preferred_element_type=F32)[:, HEAD_DIM:]
    imp_h = _dot3(pc, ov_ref[...])
    imp = functools.reduce(jnp.add, [imp_h[h * tq:(h + 1) * tq] for h in range(NSA_HEADS)])

    nsl = imp.shape[1]
    lane = lax.broadcasted_iota(jnp.int32, (tq, nsl), 1)
    jblk = (lane // 128) * (KEY_CHUNK // SEL_BLOCK) + lane % 128
    lane_ok = (lane % 128 < KEY_CHUNK // SEL_BLOCK) & (jblk < n_sel)
    cur = p_q // SEL_BLOCK
    forced = (jblk == 0) | (jblk == cur)
    score = jnp.where(forced, FORCE_SCORE, jnp.where(jblk < cur, imp, NEG_INF))
    score = jnp.where(lane_ok, score, NEG_INF)
    jf = jnp.where(lane_ok, jblk, 1 << 20).astype(F32)
    sel = _topk_select(score, jf, min(SEL_TOPK, n_sel), _block_lanes(n_sel, KEY_CHUNK // SEL_BLOCK))
    bias_r = jnp.concatenate([jnp.where(sel > 0.5, 0.0, MASK_BIAS)] * NSA_HEADS, axis=0).astype(BF16)

    klen = e0_ref.shape[1]

    def selected(ext):
        neg, pvs = [], []
        kl = lax.broadcasted_iota(jnp.int32, (r, ext), 1)
        for c in range(n_chunks):
            bias = jnp.dot(bias_r[:, c * 128:(c + 1) * 128], e0_ref[:, 0:ext], preferred_element_type=F32)
            if has_new:
                kv = slc_ref[0, :, c * klen:c * klen + ext]
                s = jnp.dot(q_rot, kv, preferred_element_type=F32)
                pvs.append(functools.partial(_dot_nt, b=kv))
            else:
                kv = slc_ref[0, c * klen:c * klen + ext, :]
                s = _dot_nt(q_rot, kv)
                pvs.append(functools.partial(lambda e, b: jnp.dot(e, b, preferred_element_type=F32), b=kv))
            neg.append(jnp.where(kl + c * klen <= p_row, s + bias, NEG_INF))
        if has_new:
            kv = newrow_ref[:, 128:256].astype(BF16)
            visible = pos0 + lax.broadcasted_iota(jnp.int32, (r, kv.shape[0]), 1) <= p_row
            neg.append(jnp.where(visible, _dot_nt(q_rot, kv), NEG_INF))
            pvs.append(functools.partial(lambda e, b: jnp.dot(e, b, preferred_element_type=F32), b=kv))
        es, inv = _softmax_parts(neg)
        o = functools.reduce(jnp.add, [pv(e.astype(BF16)) for e, pv in zip(es, pvs)])
        return (o * inv)[:, HEAD_DIM:]

    if has_new or n_chunks > 1 or klen % CAUSAL_STEP or klen == CAUSAL_STEP:
        o_slc = selected(klen)
    else:
        need = (q0 + tq + CAUSAL_STEP - 1) // CAUSAL_STEP
        for e in range(1, klen // CAUSAL_STEP + 1):
            @pl.when(need == e)
            def _(e=e):
                oslc_ref[...] = selected(e * CAUSAL_STEP)
        o_slc = oslc_ref[...]

    if has_new:
        wl = win_ref.shape[1]
        segs = [(win_ref[0].astype(BF16), pos0 - wl), (newwin_ref[...].astype(BF16), pos0)]
    else:
        span = min(t_total, WINDOW + tq)
        start = pl.multiple_of(jnp.clip(q0 - WINDOW, 0, t_total - span), tq)
        segs = [(win_ref[0, pl.ds(start, span), :].astype(BF16), pos0 + start)]
    neg = []
    for kv, kp0 in segs:
        diff = p_row - (kp0 + lax.broadcasted_iota(jnp.int32, (r, kv.shape[0]), 1))
        in_window = lax.bitcast_convert_type(diff, jnp.uint32) < WINDOW
        neg.append(jnp.where(in_window, _dot_nt(q_rot, kv), NEG_INF))
    es, inv = _softmax_parts(neg)
    o_win = functools.reduce(jnp.add, [jnp.dot(e.astype(BF16), kv, preferred_element_type=F32) for e, (kv, _) in zip(es, segs)])
    o_win = (o_win * inv)[:, HEAD_DIM:]

    g = gate_ref[...]
    outs = []
    for h in range(NSA_HEADS):
        rows = slice(h * tq, (h + 1) * tq)
        outs.append(g[:, h:h + 1] * o_cmp[rows] + g[:, 4 + h:5 + h] * o_slc[rows] + g[:, 8 + h:9 + h] * o_win[rows])
    o_ref[...] = jnp.concatenate(outs, axis=1)


def _nsa_overlap(ncp, n_cmp, n_sel, nsl):
    per = KEY_CHUNK // SEL_BLOCK
    ov = np.zeros((ncp, nsl), np.float32)
    cs = np.arange(n_cmp) * CMP_STRIDE
    js = np.arange(n_sel) * SEL_BLOCK
    hit = (cs[:, None] < js[None, :] + SEL_BLOCK) & (cs[:, None] + CMP_BLOCK > js[None, :])
    lanes = (np.arange(n_sel) // per) * 128 + np.arange(n_sel) % per
    ov[:n_cmp, lanes] = hit
    return jnp.asarray(ov, dtype=BF16)


def nsa_attend(nq_raw, nq_rot, gates, pproj, cvec, w2bd, slc, win, b, t, pos0, new_rows=None, new_win=None):
    has_new = new_rows is not None
    tq = _pick_tile(t, 128)
    nq = t // tq
    l_slc = slc.shape[2] if has_new else slc.shape[1]
    l_ctx = l_slc + (t if has_new else 0)
    assert l_slc % KEY_CHUNK == 0 or l_slc < KEY_CHUNK
    assert not has_new or (t <= SEL_BLOCK and l_slc % KEY_CHUNK == 0 and t < CMP_STRIDE and pos0 >= win.shape[1])
    n_chunks = max(l_slc // KEY_CHUNK, 1)
    n_cmp = (l_ctx - CMP_BLOCK) // CMP_STRIDE + 1
    n_sel = -(-l_ctx // SEL_BLOCK)
    per = KEY_CHUNK // SEL_BLOCK
    nsl = -(-n_sel // per) * 128
    ncp = pproj.shape[1]
    ov = _nsa_overlap(ncp, n_cmp, n_sel, nsl)
    e0 = _block_expand_matrix(SEL_BLOCK)
    if l_slc < KEY_CHUNK:
        e0 = e0[:, :l_slc]
    kern = functools.partial(_nsa_attend_kernel, tq=tq, pos0=pos0, n_cmp=n_cmp, n_sel=n_sel, n_chunks=n_chunks,
                             has_new=has_new, t_total=t)
    tile = lambda w: pl.BlockSpec((tq, w), lambda i, j: (i * nq + j, 0))
    const = lambda a: pl.BlockSpec(a.shape, lambda i, j: (0,) * a.ndim)
    batch = lambda a: pl.BlockSpec((1,) + a.shape[1:], lambda i, j: (i,) + (0,) * (a.ndim - 1))
    in_specs = [tile(256), tile(256), tile(128), batch(pproj), const(cvec), const(w2bd), const(ov), const(e0),
                batch(slc), batch(win)]
    args = [nq_raw, nq_rot, gates, pproj, cvec, w2bd, ov, e0, slc, win]
    if has_new:
        in_specs += [tile(256), tile(128)]
        args += [new_rows, new_win]
    return pl.pallas_call(
        kern, grid=(b, nq), in_specs=in_specs, out_specs=tile(256),
        out_shape=jax.ShapeDtypeStruct((b * t, GROUP_WIDTH), F32),
        scratch_shapes=[pltpu.VMEM((ncp, 128), BF16), pltpu.VMEM((NSA_HEADS * tq, HEAD_DIM), F32)],
        compiler_params=_cparams("parallel", "arbitrary"), name="nsa_attend",
    )(*args)


MOBA_PER_CHUNK = KEY_CHUNK // MOBA_BLOCK


def _moba_prep_kernel(*refs, n_src, paged):
    if paged:
        refs = refs[1:]
    srcs, avg_ref, kv_ref, mean_ref = refs[:n_src], refs[n_src], refs[n_src + 1], refs[n_src + 2]
    if paged:
        for k, src in enumerate(srcs):
            kv_ref[0, :, k * PAGE_SIZE:(k + 1) * PAGE_SIZE] = src[...].astype(BF16)
        mean_ref[0] = jnp.dot(kv_ref[0, 0:256, :], avg_ref[...], preferred_element_type=F32)
    else:
        kv_ref[0] = srcs[0][...].astype(BF16)
        mean_ref[0] = jnp.dot(avg_ref[...], kv_ref[0, :, 0:256], preferred_element_type=F32)


def moba_prep(rows=None, cache=None, layer=None, page_table=None):
    if cache is None:
        b, t, _ = rows.shape
        klen = min(KEY_CHUNK, t)
        n_chunks, n_src = t // klen, 1
        in_specs = [pl.BlockSpec((None, klen, 512), lambda i, j: (i, j, 0))]
        args, grid_kw, nsp = (rows,), {}, 0
    else:
        b, n_pages = page_table.shape
        n_src = min(KEY_CHUNK // PAGE_SIZE, n_pages)
        klen = n_src * PAGE_SIZE
        n_chunks = n_pages // n_src
        in_specs = _page_specs(n_src, 512, layer)
        args, nsp = (page_table,) + (cache,) * n_src, 1
    avg = (_block_expand_matrix(MOBA_BLOCK)[:, :klen].astype(F32) / MOBA_BLOCK).astype(BF16)
    if nsp:
        avg = avg.T
        out_specs = [pl.BlockSpec((1, 512, klen), lambda i, j, pt: (i, 0, j)),
                     pl.BlockSpec((1, 256, 128), lambda i, j, pt: (i, 0, j))]
        out_shape = [jax.ShapeDtypeStruct((b, 512, n_chunks * klen), BF16),
                     jax.ShapeDtypeStruct((b, 256, n_chunks * 128), F32)]
        cmap = lambda i, j, pt: (0, 0)
    else:
        out_specs = [pl.BlockSpec((1, klen, 512), lambda i, j: (i, j, 0)),
                     pl.BlockSpec((1, 128, 256), lambda i, j: (i, j, 0))]
        out_shape = [jax.ShapeDtypeStruct((b, n_chunks * klen, 512), BF16),
                     jax.ShapeDtypeStruct((b, n_chunks * 128, 256), F32)]
        cmap = lambda i, j: (0, 0)
    grid_spec = pltpu.PrefetchScalarGridSpec(
        num_scalar_prefetch=nsp, grid=(b, n_chunks),
        in_specs=in_specs + [pl.BlockSpec(avg.shape, cmap)], out_specs=out_specs)
    return pl.pallas_call(
        functools.partial(_moba_prep_kernel, n_src=n_src, paged=bool(nsp)), grid_spec=grid_spec, out_shape=out_shape,
        compiler_params=_cparams("parallel", "arbitrary"), name="moba_prep",
    )(*args, avg)


def _moba_attend_kernel(*refs, tq, pos0, n_cols, n_chunks, has_new):
    mq_ref, mean_ref, e0_ref, kv_ref = refs[:4]
    if has_new:
        new_ref = refs[4]
        o_ref, sel_ref, m_ref, l_ref, acc_ref = refs[5:]
    else:
        o_ref, sel_ref, m_ref, l_ref, acc_ref = refs[4:]
    qt, c = pl.program_id(1), pl.program_id(2)
    scale = HEAD_DIM ** -0.5
    r2 = 2 * tq
    klen = e0_ref.shape[1]
    q = mq_ref[...]
    q0 = pos0 + qt * tq
    p_row = q0 + lax.broadcasted_iota(jnp.int32, (r2, 1), 0) % tq

    def gate_scores(qh, means):
        return jnp.dot(qh, means, preferred_element_type=F32) if has_new else _dot_nt(qh, means)

    @pl.when(c == 0)
    def _():
        means = mean_ref[0].astype(BF16)
        nsl = n_chunks * 128
        lane = lax.broadcasted_iota(jnp.int32, (tq, nsl), 1)
        jblk = (lane // 128) * MOBA_PER_CHUNK + lane % 128
        lane_ok = (lane % 128 < MOBA_PER_CHUNK) & (jblk < n_cols)
        own = (q0 + lax.broadcasted_iota(jnp.int32, (tq, 1), 0)) // MOBA_BLOCK
        jf = jnp.where(lane_ok, jblk, 1 << 20).astype(F32)
        head_of_lane = lax.broadcasted_iota(jnp.int32, q.shape, 1) // HEAD_DIM
        for h in range(MOBA_HEADS):
            qh = jnp.where(head_of_lane == h, q, 0.0).astype(BF16)
            score = jnp.where(lane_ok, jnp.where(jblk < own, gate_scores(qh, means), NEG_INF), NEG_INF)
            sel = _topk_select(score, jf, MOBA_TOPK, _block_lanes(n_cols, MOBA_PER_CHUNK))
            bias = jnp.where(sel > 0.5, 0.0, jnp.where(jblk == own, 0.0, MASK_BIAS))
            for cc in range(n_chunks):
                sel_ref[cc, h * tq:(h + 1) * tq, :] = bias[:, cc * 128:(cc + 1) * 128]
        m_ref[...] = jnp.full(m_ref.shape, NEG_INF, F32)
        l_ref[...] = jnp.zeros(l_ref.shape, F32)
        acc_ref[...] = jnp.zeros(acc_ref.shape, F32)

    lane128 = lax.broadcasted_iota(jnp.int32, (tq, 128), 1)

    def pair_q(pr):
        x = q[:, 128 * pr:128 * (pr + 1)] * scale
        return jnp.concatenate([jnp.where(lane128 < HEAD_DIM, x, 0.0), jnp.where(lane128 >= HEAD_DIM, x, 0.0)],
                               axis=0).astype(BF16)

    def update(pr, neg, v, v_feature_major=False):
        m_old = m_ref[pr][:, 0:1]
        m_new = jnp.maximum(m_old, jnp.max(neg, axis=-1, keepdims=True))
        m_safe = jnp.where(m_new == NEG_INF, 0.0, m_new)
        alpha = jnp.exp(m_old - m_safe)
        e = jnp.exp(neg - m_safe)
        pv = _dot_nt(e.astype(BF16), v) if v_feature_major else jnp.dot(e.astype(BF16), v, preferred_element_type=F32)
        l_ref[pr] = alpha * l_ref[pr] + jnp.sum(e, axis=-1, keepdims=True)
        acc_ref[pr] = alpha * acc_ref[pr] + pv
        m_ref[pr] = jnp.broadcast_to(m_new, (r2, 128))

    own_row = p_row // MOBA_BLOCK

    def attend(ext):
        kpos = c * klen + lax.broadcasted_iota(jnp.int32, (r2, ext), 1)
        for pr in range(2):
            bias = jnp.dot(sel_ref[c, 2 * pr * tq:(2 * pr + 2) * tq, :].astype(BF16), e0_ref[:, 0:ext],
                           preferred_element_type=F32)
            if has_new:
                s = jnp.dot(pair_q(pr), kv_ref[0, 128 * pr:128 * (pr + 1), 0:ext], preferred_element_type=F32)
                neg = jnp.where(kpos <= p_row, s + bias, NEG_INF)
                update(pr, neg, kv_ref[0, 256 + 128 * pr:256 + 128 * (pr + 1), 0:ext], v_feature_major=True)
            else:
                s = _dot_nt(pair_q(pr), kv_ref[0, 0:ext, 128 * pr:128 * (pr + 1)])
                neg = jnp.where(kpos <= p_row, s + bias, NEG_INF)
                update(pr, neg, kv_ref[0, 0:ext, 256 + 128 * pr:256 + 128 * (pr + 1)])

    if has_new or n_chunks > 1 or klen % CAUSAL_STEP or klen == CAUSAL_STEP:
        attend(klen)
    else:
        need = (qt * tq + tq + CAUSAL_STEP - 1) // CAUSAL_STEP
        for e in range(1, klen // CAUSAL_STEP + 1):
            @pl.when(need == e)
            def _(e=e):
                attend(e * CAUSAL_STEP)

    @pl.when(c == n_chunks - 1)
    def _():
        outs = []
        for pr in range(2):
            if has_new:
                t_new = new_ref.shape[0]
                kp = pos0 + lax.broadcasted_iota(jnp.int32, (r2, t_new), 1)
                s = _dot_nt(pair_q(pr), new_ref[:, 128 * pr:128 * (pr + 1)].astype(BF16))
                neg = jnp.where(kp <= p_row, jnp.where(kp // MOBA_BLOCK == own_row, s, NEG_INF), NEG_INF)
                update(pr, neg, new_ref[:, 256 + 128 * pr:256 + 128 * (pr + 1)].astype(BF16))
            o = acc_ref[pr] / jnp.maximum(l_ref[pr][:, 0:1], 1e-30)
            outs += [o[:tq, :HEAD_DIM], o[tq:, HEAD_DIM:]]
        o_ref[...] = jnp.concatenate(outs, axis=1)


def moba_attend(mq, means, kv, b, t, pos0, new_rows=None):
    has_new = new_rows is not None
    tq = _pick_tile(t, 128)
    nq = t // tq
    l_kv = kv.shape[2] if has_new else kv.shape[1]
    klen = min(KEY_CHUNK, l_kv)
    n_chunks = l_kv // klen
    n_full = (l_kv + (t if has_new else 0)) // MOBA_BLOCK
    assert n_full >= MOBA_TOPK and l_kv % klen == 0 and klen % MOBA_BLOCK == 0
    assert not has_new or t <= MOBA_BLOCK
    e0 = _block_expand_matrix(MOBA_BLOCK)[:, :klen]
    kern = functools.partial(_moba_attend_kernel, tq=tq, pos0=pos0, n_cols=n_full, n_chunks=n_chunks, has_new=has_new)
    tile = lambda w: pl.BlockSpec((tq, w), lambda i, j, c: (i * nq + j, 0))
    kv_spec = (pl.BlockSpec((1, 512, klen), lambda i, j, c: (i, 0, c)) if has_new
               else pl.BlockSpec((1, klen, 512), lambda i, j, c: (i, c, 0)))
    in_specs = [tile(256), pl.BlockSpec((1,) + means.shape[1:], lambda i, j, c: (i, 0, 0)),
                pl.BlockSpec(e0.shape, lambda i, j, c: (0, 0)), kv_spec]
    args = [mq, means, e0, kv]
    if has_new:
        in_specs.append(tile(512))
        args.append(new_rows)
    return pl.pallas_call(
        kern, grid=(b, nq, n_chunks), in_specs=in_specs, out_specs=tile(256),
        out_shape=jax.ShapeDtypeStruct((b * t, GROUP_WIDTH), F32),
        scratch_shapes=[pltpu.VMEM((n_chunks, MOBA_HEADS * tq, 128), F32), pltpu.VMEM((2, 2 * tq, 128), F32),
                        pltpu.VMEM((2, 2 * tq, 128), F32), pltpu.VMEM((2, 2 * tq, 128), F32)],
        compiler_params=_cparams("parallel", "arbitrary", "arbitrary"), name="moba_attend",
    )(*args)


def _layer(h, p_emb, b, t, pos0, lw, past, final_gain, final):
    prompt = past is None
    pos = pos0 + jnp.arange(t, dtype=jnp.int32)
    h = ffn_block(h, lw['norm_ffn1'], lw['ffn1_w_in'], lw['ffn1_w_out'])
    (gla, loga, nq_raw, nq_rot, nsa_rows, win_rows, gates, su, mq, moba_rows) = mixin_block(
        h, lw['norm_mix'], lw['wcat'], _rope_table(pos), lw['gla_gate_w'], lw['gla_gate_b'], t)

    s0 = jnp.zeros((b, GLA_HEADS * GLA_DK, GLA_DV), F32) if prompt else past['gla'].reshape(b, -1, GLA_DV)
    o_gla, gla_state = gla_block(gla, loga, s0, lw['gla_norm'], b, GLA_CHUNK if prompt else t)
    gla_state = gla_state.reshape(b, GLA_HEADS, GLA_DK, GLA_DV)

    rows3 = nsa_rows.reshape(b, t, 4 * HEAD_DIM)
    win3 = win_rows.reshape(b, t, 2 * HEAD_DIM)
    if prompt:
        slc, pproj = nsa_prep(lw['nsa_wc'], rows=rows3)
        o_nsa = nsa_attend(nq_raw, nq_rot, gates, pproj, lw['nsa_cvec'], lw['nsa_w2bd'], slc, win3, b, t, 0)
        new_win = win3[:, -min(WINDOW, t):]
    else:
        slc, pproj = nsa_prep(lw['nsa_wc'], cache=past['nsa_cache'], layer=past['layer'], page_table=past['page_table'])
        wb = past['win'].reshape(b, -1, 2 * HEAD_DIM)
        o_nsa = nsa_attend(nq_raw, nq_rot, gates, pproj, lw['nsa_cvec'], lw['nsa_w2bd'], slc, wb, b, t, pos0,
                           new_rows=nsa_rows, new_win=win_rows)
        new_win = jnp.concatenate([wb, win3], axis=1)[:, -wb.shape[1]:]

    if prompt:
        h0 = jnp.zeros((b, 1, 2 * S5_LANES), F32)
    else:
        h0 = jnp.concatenate([past['s5_re'].reshape(b, 1, -1), past['s5_im'].reshape(b, 1, -1)], axis=-1)
    o_s5, s5_re, s5_im = s5_block(su, h0, lw['s5_pw'], lw['s5_bcat'], lw['s5_ccat'], lw['s5_d'],
                                  lw['s5_glu_w'], lw['s5_glu_b'], b)

    mrows = moba_rows.reshape(b, t, 2, MOBA_HEADS, HEAD_DIM)
    if prompt:
        mkv, means = moba_prep(rows=moba_rows.reshape(b, t, 512))
        o_moba = moba_attend(mq, means, mkv, b, t, 0)
    else:
        mkv, means = moba_prep(cache=past['moba_cache'], layer=past['layer'], page_table=past['page_table'])
        o_moba = moba_attend(mq, means, mkv, b, t, pos0, new_rows=moba_rows)

    h = mixout_block(h, (o_gla, o_nsa, o_s5, o_moba), lw['w_mix_out'])
    h = ffn_block(h, lw['norm_ffn2'], lw['ffn2_w_in'], lw['ffn2_w_out'])
    h = ple_block(h, p_emb, lw['norm_ple'], lw['ple_w'], lw['ple_gate_w'], final_gain, final)
    state = {'nsa': nsa_rows.reshape(b, t, 4, HEAD_DIM), 'moba': mrows,
             'win': new_win.reshape(b, -1, 2, HEAD_DIM), 'gla': gla_state, 's5_re': s5_re, 's5_im': s5_im}
    return h, state


def kernel(x_prompt, x_sample, p_prompt, p_sample, cache_nsa_kv, cache_moba_kv, state_nsa_win, state_gla,
           state_s5_re, state_s5_im, page_table, norm_ffn1, ffn1_w_in, ffn1_w_out, norm_mix, w_mix_in, w_mix_out,
           gla_gate_w, gla_gate_b, gla_norm, nsa_cmp_pe, nsa_cmp_w1, nsa_cmp_w2, s5_a_re, s5_a_im, s5_b_re, s5_b_im,
           s5_c_re, s5_c_im, s5_d, s5_log_dt, s5_glu_w, s5_glu_b, norm_ffn2, ffn2_w_in, ffn2_w_out, norm_ple, ple_w,
           ple_gate_w, norm_final):
    bp, tp, d = x_prompt.shape
    bs, ts, _ = x_sample.shape
    depth = p_prompt.shape[0]
    past_len = page_table.shape[1] * PAGE_SIZE
    params = {'norm_ffn1': norm_ffn1, 'norm_mix': norm_mix, 'gla_gate_w': gla_gate_w, 'gla_gate_b': gla_gate_b,
              'gla_norm': gla_norm, 'nsa_cmp_pe': nsa_cmp_pe, 'nsa_cmp_w1': nsa_cmp_w1, 'nsa_cmp_w2': nsa_cmp_w2,
              's5_a_re': s5_a_re, 's5_a_im': s5_a_im, 's5_b_re': s5_b_re, 's5_b_im': s5_b_im, 's5_c_re': s5_c_re,
              's5_c_im': s5_c_im, 's5_d': s5_d, 's5_log_dt': s5_log_dt, 's5_glu_w': s5_glu_w, 's5_glu_b': s5_glu_b,
              'norm_ffn2': norm_ffn2, 'norm_ple': norm_ple}
    hp = x_prompt.reshape(bp * tp, d)
    hs = x_sample.reshape(bs * ts, d)
    nsa_cache = jnp.transpose(cache_nsa_kv, (0, 1, 3, 4, 2)).reshape(cache_nsa_kv.shape[:2] + (4 * HEAD_DIM, PAGE_SIZE))
    moba_cache = jnp.transpose(cache_moba_kv, (0, 1, 3, 4, 5, 2)).reshape(
        cache_moba_kv.shape[:2] + (2 * MOBA_HEADS * HEAD_DIM, PAGE_SIZE))
    st_p, st_s = [], []
    for i in range(depth):
        lw = {k: v[i] for k, v in params.items()}
        lw['wcat'] = _mixin_weights(w_mix_in[i])
        for name, w in (('ffn1_w_in', ffn1_w_in), ('ffn1_w_out', ffn1_w_out), ('ffn2_w_in', ffn2_w_in),
                        ('ffn2_w_out', ffn2_w_out), ('w_mix_out', w_mix_out), ('ple_w', ple_w),
                        ('ple_gate_w', ple_gate_w)):
            lw[name] = w[i].astype(BF16)
        lw['s5_pw'], lw['s5_bcat'], lw['s5_ccat'] = _s5_params(s5_a_re[i], s5_a_im[i], s5_b_re[i], s5_b_im[i],
                                                               s5_c_re[i], s5_c_im[i], s5_log_dt[i])
        lw['s5_glu_w'] = s5_glu_w[i].astype(BF16)
        lw['nsa_wc'] = _nsa_cmp_weights(nsa_cmp_w1[i])
        pe, w1, w2 = nsa_cmp_pe[i], nsa_cmp_w1[i], nsa_cmp_w2[i]
        lw['nsa_cvec'] = jnp.concatenate([jnp.dot(pe[s].reshape(1, -1), w1[s], precision=HIGHEST) for s in range(2)], axis=1)
        zw = jnp.zeros_like(w2[0])
        lw['nsa_w2bd'] = jnp.concatenate([jnp.concatenate([w2[0], zw], axis=1),
                                          jnp.concatenate([zw, w2[1]], axis=1)], axis=0).astype(BF16)
        final = i == depth - 1
        hp, sp = _layer(hp, p_prompt[i].reshape(bp * tp, -1), bp, tp, 0, lw, None, norm_final, final)
        past = {'nsa_cache': nsa_cache, 'moba_cache': moba_cache, 'layer': i, 'page_table': page_table,
                'win': state_nsa_win[i], 'gla': state_gla[i], 's5_re': state_s5_re[i], 's5_im': state_s5_im[i]}
        hs, ss = _layer(hs, p_sample[i].reshape(bs * ts, -1), bs, ts, past_len, lw, past, norm_final, final)
        st_p.append(sp)
        st_s.append(ss)

    def stack(states, name):
        return jnp.stack([s[name] for s in states], axis=0)

    return (hp.reshape(bp, tp, d), hs.reshape(bs, ts, d),
            stack(st_p, 'nsa'), stack(st_s, 'nsa'), stack(st_p, 'moba'), stack(st_s, 'moba'),
            stack(st_p, 'win'), stack(st_s, 'win'), stack(st_p, 'gla'), stack(st_s, 'gla'),
            stack(st_p, 's5_re'), stack(st_s, 's5_re'), stack(st_p, 's5_im'), stack(st_s, 's5_im'))
```

```python
import functools
import math

import numpy as np
import jax
import jax.numpy as jnp
from jax import lax
from jax.experimental import pallas as pl
from jax.experimental.pallas import tpu as pltpu

F32 = jnp.float32
BF16 = jnp.bfloat16

HEAD_DIM = 64
ROPE_THETA = 10000.0
RMS_EPS = 1e-6
PAGE_SIZE = 128
GLA_HEADS = 4
GLA_DK = 32
GLA_DV = 64
GLA_GATE_RANK = 16
GLA_TAU = 16.0
GLA_CHUNK = 32
NSA_HEADS = 4
CMP_BLOCK = 32
CMP_STRIDE = 16
SEL_BLOCK = 64
SEL_TOPK = 16
WINDOW = 512
FORCE_SCORE = 1e4
S5_GROUP_SIZE = 16
S5_GROUPS = 16
S5_STATE = 64
MOBA_HEADS = 4
MOBA_BLOCK = 256
MOBA_TOPK = 3
GROUP_WIDTH = 256

VMEM_LIMIT_BYTES = 48 * 1024 * 1024


def _cparams(*sem):
    return pltpu.CompilerParams(dimension_semantics=sem, vmem_limit_bytes=VMEM_LIMIT_BYTES)


def _rms(x, g):
    return x * lax.rsqrt(jnp.mean(x * x, axis=-1, keepdims=True) + RMS_EPS) * g


def _pick_tile(m, pref):
    t = min(m, pref)
    while m % t:
        t //= 2
    return t


def _ffn_kernel(x_ref, g_ref, wi_ref, wo_ref, o_ref, xn_ref, acc_ref, *, f, tf):
    j = pl.program_id(1)

    @pl.when(j == 0)
    def _():
        xn_ref[...] = _rms(x_ref[...], g_ref[...]).astype(BF16)
        acc_ref[...] = jnp.zeros_like(acc_ref)

    xn = xn_ref[...]
    c_gate = pl.multiple_of(j * tf, tf)
    c_up = pl.multiple_of(f + j * tf, tf)
    g = jnp.dot(xn, wi_ref[:, pl.ds(c_gate, tf)], preferred_element_type=F32)
    u = jnp.dot(xn, wi_ref[:, pl.ds(c_up, tf)], preferred_element_type=F32)
    act = (g * jax.nn.sigmoid(g) * u).astype(BF16)
    acc_ref[...] += jnp.dot(act, wo_ref[pl.ds(c_gate, tf), :], preferred_element_type=F32)

    @pl.when(j == pl.num_programs(1) - 1)
    def _():
        o_ref[...] = x_ref[...] + 0.5 * acc_ref[...]


def ffn_block(x, gain, w_in, w_out):
    m, d = x.shape
    f = w_out.shape[0]
    tm = _pick_tile(m, 1024)
    tf = 256
    assert f % tf == 0
    resident = lambda a: pl.BlockSpec(a.shape, lambda i, j: (0, 0), pipeline_mode=pl.Buffered(1))
    return pl.pallas_call(
        functools.partial(_ffn_kernel, f=f, tf=tf),
        grid=(m // tm, f // tf),
        in_specs=[
            pl.BlockSpec((tm, d), lambda i, j: (i, 0)),
            pl.BlockSpec((1, d), lambda i, j: (0, 0)),
            resident(w_in),
            resident(w_out),
        ],
        out_specs=pl.BlockSpec((tm, d), lambda i, j: (i, 0)),
        out_shape=jax.ShapeDtypeStruct((m, d), F32),
        scratch_shapes=[pltpu.VMEM((tm, d), BF16), pltpu.VMEM((tm, d), F32)],
        compiler_params=_cparams("parallel", "arbitrary"),
        name="ffn_block",
    )(x, gain.reshape(1, d), w_in, w_out)


MIXIN_OUT_WIDTHS = (768, 128, 256, 256, 256, 128, 128, 256, 256, 512)


def _mixin_kernel(h_ref, g_ref, w_ref, tab_ref, gw_ref, gb_ref,
                  gla_o, loga_o, nqraw_o, nqrot_o, nsa_o, win_o, gate_o, su_o, mq_o, moba_o):
    xn = _rms(h_ref[...], g_ref[...]).astype(BF16)

    def mm(a, b):
        return jnp.dot(xn, w_ref[:, a:b], preferred_element_type=F32)

    c256, s256 = tab_ref[:, 0:256], tab_ref[:, 256:512]
    gla_o[...] = mm(0, 768)
    z = mm(768, 896)
    la = jnp.dot(z, gw_ref[...], preferred_element_type=F32) + gb_ref[...]
    loga_o[...] = (jnp.minimum(la, 0.0) - jnp.log(1.0 + jnp.exp(-jnp.abs(la)))) * (1.0 / GLA_TAU)
    nq = mm(896, 1152)
    nqraw_o[...] = nq
    nqrot_o[...] = nq * c256 + mm(1152, 1408) * s256
    nsa_o[...] = mm(1408, 1664) * tab_ref[:, 512:768] + mm(1792, 2048) * tab_ref[:, 768:1024]
    win_o[...] = mm(1664, 1792) * tab_ref[:, 1024:1152] + mm(2048, 2176) * tab_ref[:, 1152:1280]
    gate_o[...] = jax.nn.sigmoid(mm(2176, 2304))
    su_o[...] = mm(2304, 2560)
    mq_o[...] = mm(2560, 2816) * c256 + mm(2816, 3072) * s256
    moba_o[:, 0:256] = mm(3072, 3328) * c256 + mm(3328, 3584) * s256
    moba_o[:, 256:512] = mm(3584, 3840)


def _rot_cols(w):
    d, n = w.shape
    w = w.reshape(d, n // HEAD_DIM, 2, HEAD_DIM // 2)
    return jnp.stack([-w[:, :, 1], w[:, :, 0]], axis=2).reshape(d, n)


def _mixin_weights(w):
    d = w.shape[0]
    z = lambda n: jnp.zeros((d, n), w.dtype)
    gq, gk, gv, gz, gog = w[:, 0:128], w[:, 128:256], w[:, 256:512], w[:, 512:528], w[:, 528:784]
    nq, nkv, ng, su = w[:, 784:1040], w[:, 1040:1424], w[:, 1424:1436], w[:, 1436:1692]
    mq, mk, mv = w[:, 1692:1948], w[:, 1948:2204], w[:, 2204:2460]
    r_nsa = jnp.concatenate([z(128), _rot_cols(nkv[:, 128:192]), z(64)], axis=1)
    r_win = jnp.concatenate([_rot_cols(nkv[:, 256:320]), z(64)], axis=1)
    cat = jnp.concatenate([gq, gk, gv, gog, gz, z(112), nq, _rot_cols(nq), nkv, r_nsa, r_win, ng, z(116), su,
                           mq, _rot_cols(mq), mk, _rot_cols(mk), mv], axis=1)
    return cat.astype(BF16)


def _rope_table(pos):
    half = HEAD_DIM // 2
    inv = ROPE_THETA ** (-jnp.arange(half, dtype=F32) / half)
    ang = pos.astype(F32)[:, None] * inv
    c, s = jnp.cos(ang), jnp.sin(ang)
    c64, s64 = jnp.concatenate([c, c], -1), jnp.concatenate([s, s], -1)
    one, zero = jnp.ones_like(c64), jnp.zeros_like(c64)
    return jnp.concatenate([c64, c64, c64, c64, s64, s64, s64, s64, one, one, c64, one, zero, zero, s64, zero,
                            c64, one, s64, zero], axis=-1)


def mixin_block(h, gain, wcat, table, gate_w, gate_b, t_len):
    m, d = h.shape
    tm = _pick_tile(m, 512)
    if t_len % tm == 0:
        ntab = t_len // tm
        tab_map = lambda i: (i % ntab, 0)
    else:
        table = jnp.tile(table, (tm // t_len, 1))
        tab_map = lambda i: (0, 0)
    gw = jnp.zeros((128, 128), F32).at[:GLA_GATE_RANK].set(gate_w)
    return pl.pallas_call(
        _mixin_kernel,
        grid=(m // tm,),
        in_specs=[
            pl.BlockSpec((tm, d), lambda i: (i, 0)),
            pl.BlockSpec((1, d), lambda i: (0, 0)),
            pl.BlockSpec(wcat.shape, lambda i: (0, 0)),
            pl.BlockSpec((tm, table.shape[1]), tab_map),
            pl.BlockSpec((128, 128), lambda i: (0, 0)),
            pl.BlockSpec((1, 128), lambda i: (0, 0)),
        ],
        out_specs=[pl.BlockSpec((tm, n), lambda i: (i, 0)) for n in MIXIN_OUT_WIDTHS],
        out_shape=[jax.ShapeDtypeStruct((m, n), F32) for n in MIXIN_OUT_WIDTHS],
        compiler_params=_cparams("parallel"),
        name="mixin_block",
    )(h, gain.reshape(1, d), wcat, table, gw, gate_b.reshape(1, 128))


def _mixout_kernel(h_ref, a_ref, b_ref, c_ref, d_ref, w_ref, o_ref):
    acc = h_ref[...]
    for k, r in enumerate((a_ref, b_ref, c_ref, d_ref)):
        acc = acc + jnp.dot(r[...].astype(BF16), w_ref[k * GROUP_WIDTH:(k + 1) * GROUP_WIDTH, :],
                            preferred_element_type=F32)
    o_ref[...] = acc


def mixout_block(h, parts, w):
    m, d = h.shape
    tm = _pick_tile(m, 512)
    return pl.pallas_call(
        _mixout_kernel,
        grid=(m // tm,),
        in_specs=[pl.BlockSpec((tm, d), lambda i: (i, 0))]
        + [pl.BlockSpec((tm, GROUP_WIDTH), lambda i: (i, 0)) for _ in range(4)]
        + [pl.BlockSpec((d, d), lambda i: (0, 0))],
        out_specs=pl.BlockSpec((tm, d), lambda i: (i, 0)),
        out_shape=jax.ShapeDtypeStruct((m, d), F32),
        compiler_params=_cparams("parallel"),
        name="mixout_block",
    )(h, *parts, w)


def _ple_kernel(h_ref, p_ref, g_ref, wp_ref, wg_ref, gf_ref, o_ref, *, final):
    h = h_ref[...]
    gate = jax.nn.sigmoid(jnp.dot(_rms(h, g_ref[...]).astype(BF16), wg_ref[...], preferred_element_type=F32))
    e = jnp.dot(p_ref[...].astype(BF16), wp_ref[...], preferred_element_type=F32)
    y = h + e * gate
    o_ref[...] = _rms(y, gf_ref[...]) if final else y


def ple_block(h, p, gain, ple_w, gate_w, final_gain, final):
    m, d = h.shape
    pd = p.shape[1]
    tm = _pick_tile(m, 512)
    return pl.pallas_call(
        functools.partial(_ple_kernel, final=final),
        grid=(m // tm,),
        in_specs=[
            pl.BlockSpec((tm, d), lambda i: (i, 0)),
            pl.BlockSpec((tm, pd), lambda i: (i, 0)),
            pl.BlockSpec((1, d), lambda i: (0, 0)),
            pl.BlockSpec((pd, d), lambda i: (0, 0)),
            pl.BlockSpec((d, d), lambda i: (0, 0)),
            pl.BlockSpec((1, d), lambda i: (0, 0)),
        ],
        out_specs=pl.BlockSpec((tm, d), lambda i: (i, 0)),
        out_shape=jax.ShapeDtypeStruct((m, d), F32),
        compiler_params=_cparams("parallel"),
        name="ple_block",
    )(h, p, gain.reshape(1, d), ple_w, gate_w, final_gain.reshape(1, d))


S5_LANES = S5_GROUPS * S5_STATE
S5_TILE = 8
HIGHEST = lax.Precision.HIGHEST


def _s5_kernel(u_ref, h0_ref, pw_ref, bcat_ref, ccat_ref, d_ref, gw_ref, gb_ref,
               o_ref, hre_ref, him_ref, bu_ref, h_ref, carry_ref):
    n = S5_LANES
    tb = u_ref.shape[0]

    @pl.when(pl.program_id(1) == 0)
    def _():
        carry_ref[...] = jnp.broadcast_to(h0_ref[0], (S5_TILE, 2 * n))

    u = u_ref[...]
    bu_ref[...] = jnp.dot(u.astype(BF16), bcat_ref[...], preferred_element_type=F32)

    def tile(k, carry):
        cr, ci = carry
        r0 = pl.multiple_of(k * S5_TILE, S5_TILE)
        xr = bu_ref[pl.ds(r0, S5_TILE), 0:n]
        xi = bu_ref[pl.ds(r0, S5_TILE), n:2 * n]
        for lvl, sh in enumerate((1, 2, 4)):
            pr = pw_ref[:, (2 * lvl) * n:(2 * lvl + 1) * n]
            pi = pw_ref[:, (2 * lvl + 1) * n:(2 * lvl + 2) * n]
            rr = pltpu.roll(xr, sh, 0)
            ri = pltpu.roll(xi, sh, 0)
            xr, xi = xr + pr * rr - pi * ri, xi + pr * ri + pi * rr
        pr, pi = pw_ref[:, 6 * n:7 * n], pw_ref[:, 7 * n:8 * n]
        xr, xi = xr + pr * cr - pi * ci, xi + pr * ci + pi * cr
        h_ref[pl.ds(r0, S5_TILE), 0:n] = xr
        h_ref[pl.ds(r0, S5_TILE), n:2 * n] = xi
        last_r = jnp.broadcast_to(xr[S5_TILE - 1:S5_TILE, :], (S5_TILE, n))
        last_i = jnp.broadcast_to(xi[S5_TILE - 1:S5_TILE, :], (S5_TILE, n))
        return last_r, last_i

    cr, ci = lax.fori_loop(0, tb // S5_TILE, tile, (carry_ref[:, 0:n], carry_ref[:, n:2 * n]),
                           unroll=min(4, tb // S5_TILE))
    carry_ref[:, 0:n] = cr
    carry_ref[:, n:2 * n] = ci
    hre_ref[0] = cr[0:1, :]
    him_ref[0] = ci[0:1, :]

    y = jnp.dot(h_ref[...].astype(BF16), ccat_ref[...], preferred_element_type=F32) + d_ref[...] * u
    g = jax.nn.gelu(y)
    gate = jnp.dot(g.astype(BF16), gw_ref[...], preferred_element_type=F32) + gb_ref[...]
    o_ref[...] = g * jax.nn.sigmoid(gate)


def _s5_params(a_re, a_im, b_re, b_im, c_re, c_im, log_dt):
    g, n = a_re.shape
    c = b_re.shape[-1]
    dt = jnp.exp(log_dt)[:, None]
    mag = jnp.exp(a_re * dt)
    abar_re, abar_im = mag * jnp.cos(a_im * dt), mag * jnp.sin(a_im * dt)
    den = a_re * a_re + a_im * a_im
    fr = ((abar_re - 1.0) * a_re + abar_im * a_im) / den
    fi = (abar_im * a_re - (abar_re - 1.0) * a_im) / den
    bbar_re = fr[..., None] * b_re - fi[..., None] * b_im
    bbar_im = fr[..., None] * b_im + fi[..., None] * b_re
    eye = jnp.eye(g, dtype=F32)
    blk_in = lambda m: jnp.einsum('gnc,gh->gchn', m, eye).reshape(g * c, g * n)
    blk_out = lambda m: jnp.einsum('gcn,gh->gnhc', m, eye).reshape(g * n, g * c)
    bcat = jnp.concatenate([blk_in(bbar_re), blk_in(bbar_im)], axis=1).astype(BF16)
    ccat = jnp.concatenate([blk_out(c_re), -blk_out(c_im)], axis=0).astype(BF16)
    ar, ai = abar_re.reshape(1, g * n), abar_im.reshape(1, g * n)
    pows = [(ar, ai)]
    for _ in range(S5_TILE - 1):
        pr, pi = pows[-1]
        pows.append((pr * ar - pi * ai, pr * ai + pi * ar))
    t = jnp.arange(S5_TILE)[:, None]
    cols = []
    for sh in (1, 2, 4):
        pr, pi = pows[sh - 1]
        cols += [jnp.where(t >= sh, pr, 0.0), jnp.where(t >= sh, pi, 0.0)]
    cols += [jnp.concatenate([p[0] for p in pows], axis=0), jnp.concatenate([p[1] for p in pows], axis=0)]
    return jnp.concatenate(cols, axis=1), bcat, ccat


def s5_block(su, h0, pw, bcat, ccat, dvec, glu_w, glu_b, b):
    m = su.shape[0]
    t = m // b
    tb = _pick_tile(t, 256)
    nt = t // tb
    n = S5_LANES
    out, hre, him = pl.pallas_call(
        _s5_kernel,
        grid=(b, nt),
        in_specs=[
            pl.BlockSpec((tb, GROUP_WIDTH), lambda i, j: (i * nt + j, 0)),
            pl.BlockSpec((1, 1, 2 * n), lambda i, j: (i, 0, 0)),
            pl.BlockSpec(pw.shape, lambda i, j: (0, 0)),
            pl.BlockSpec(bcat.shape, lambda i, j: (0, 0)),
            pl.BlockSpec(ccat.shape, lambda i, j: (0, 0)),
            pl.BlockSpec((1, GROUP_WIDTH), lambda i, j: (0, 0)),
            pl.BlockSpec((GROUP_WIDTH, GROUP_WIDTH), lambda i, j: (0, 0)),
            pl.BlockSpec((1, GROUP_WIDTH), lambda i, j: (0, 0)),
        ],
        out_specs=[
            pl.BlockSpec((tb, GROUP_WIDTH), lambda i, j: (i * nt + j, 0)),
            pl.BlockSpec((1, 1, n), lambda i, j: (i, 0, 0)),
            pl.BlockSpec((1, 1, n), lambda i, j: (i, 0, 0)),
        ],
        out_shape=[jax.ShapeDtypeStruct((m, GROUP_WIDTH), F32),
                   jax.ShapeDtypeStruct((b, 1, n), F32), jax.ShapeDtypeStruct((b, 1, n), F32)],
        scratch_shapes=[pltpu.VMEM((tb, 2 * n), F32), pltpu.VMEM((tb, 2 * n), F32), pltpu.VMEM((S5_TILE, 2 * n), F32)],
        compiler_params=_cparams("parallel", "arbitrary"),
        name="s5_block",
    )(su, h0, pw, bcat, ccat, dvec.reshape(1, -1), glu_w, glu_b.reshape(1, -1))
    return out, hre.reshape(b, S5_GROUPS, S5_STATE), him.reshape(b, S5_GROUPS, S5_STATE)


def _gla_consts(c):
    dk = GLA_DK
    tril = np.tril(np.ones((c, c), np.float32))
    lane_j = np.arange(c * dk) // dk
    diag = (np.arange(c)[:, None] == lane_j[None, :]).astype(np.float32)
    causal = (lane_j[None, :] <= np.arange(c)[:, None]).astype(np.float32)
    gsum = (lane_j[:, None] == np.arange(c)[None, :]).astype(np.float32)
    return jnp.asarray(tril), jnp.asarray(diag), jnp.asarray(causal), jnp.asarray(gsum, dtype=BF16)


def _gla_kernel(x_ref, la_ref, s0_ref, tril_ref, diag_ref, causal_ref, gsum_ref, gn_ref,
                o_ref, sfin_ref, s_ref, *, chunk):
    c, dk, dv = chunk, GLA_DK, GLA_DV
    tg = x_ref.shape[0]

    @pl.when(pl.program_id(1) == 0)
    def _():
        s_ref[...] = s0_ref[0]

    lane_grp = lax.broadcasted_iota(jnp.int32, (c, 128), 1) // dk

    def tile_head(x, h):
        v = x
        for s in range(1, GLA_HEADS):
            v = jnp.where(lane_grp == (h + s) % GLA_HEADS, pltpu.roll(x, s * dk, 1), v)
        return jnp.concatenate([v] * (c * dk // 128), axis=1)

    def body(ci, _):
        r0 = pl.multiple_of(ci * c, c)
        q = x_ref[pl.ds(r0, c), 0:128] * dk ** -0.5
        k = x_ref[pl.ds(r0, c), 128:256]
        la = la_ref[pl.ds(r0, c), :]
        b = jnp.dot(tril_ref[...], la, precision=HIGHEST, preferred_element_type=F32)
        bt = b.T
        kt = k.T
        b_last_t = bt[:, c - 1:c]
        kdt = kt * jnp.exp(b_last_t - bt)
        qd = q * jnp.exp(b)
        diag, causal = diag_ref[...], causal_ref[...]
        for h in range(GLA_HEADS):
            v = x_ref[pl.ds(r0, c), 256 + h * dv:256 + (h + 1) * dv]
            og = x_ref[pl.ds(r0, c), 512 + h * dv:512 + (h + 1) * dv]
            s_h = s_ref[h * dk:(h + 1) * dk, :]
            b_i, q_i, k_i = tile_head(b, h), tile_head(q, h), tile_head(k, h)
            b_j = jnp.sum(b_i * diag, axis=0, keepdims=True)
            k_j = jnp.sum(k_i * diag, axis=0, keepdims=True)
            p = q_i * k_j * jnp.exp(jnp.minimum(b_i - b_j, 0.0)) * causal
            att = jnp.dot(p.astype(BF16), gsum_ref[...], preferred_element_type=F32)
            o = jnp.dot(att.astype(BF16), v.astype(BF16), preferred_element_type=F32)
            o = o + jnp.dot(qd[:, h * dk:(h + 1) * dk].astype(BF16), s_h.astype(BF16), preferred_element_type=F32)
            s_ref[h * dk:(h + 1) * dk, :] = jnp.exp(b_last_t[h * dk:(h + 1) * dk, :]) * s_h + jnp.dot(
                kdt[h * dk:(h + 1) * dk, :].astype(BF16), v.astype(BF16), preferred_element_type=F32)
            o = o * lax.rsqrt(jnp.mean(o * o, axis=-1, keepdims=True) + RMS_EPS) * gn_ref[...]
            o_ref[pl.ds(r0, c), h * dv:(h + 1) * dv] = o * (og * jax.nn.sigmoid(og))
        return 0

    lax.fori_loop(0, tg // c, body, 0, unroll=min(4, tg // c))

    @pl.when(pl.program_id(1) == pl.num_programs(1) - 1)
    def _():
        sfin_ref[0] = s_ref[...]


def gla_block(gla, loga, s0, gnorm, b, chunk):
    m = gla.shape[0]
    t = m // b
    tg = _pick_tile(t, 256)
    nt = t // tg
    consts = _gla_consts(chunk)
    hk = GLA_HEADS * GLA_DK
    out, sfin = pl.pallas_call(
        functools.partial(_gla_kernel, chunk=chunk),
        grid=(b, nt),
        in_specs=[
            pl.BlockSpec((tg, 768), lambda i, j: (i * nt + j, 0)),
            pl.BlockSpec((tg, 128), lambda i, j: (i * nt + j, 0)),
            pl.BlockSpec((1, hk, GLA_DV), lambda i, j: (i, 0, 0)),
        ] + [pl.BlockSpec(a.shape, lambda i, j: (0, 0)) for a in consts]
        + [pl.BlockSpec((1, GLA_DV), lambda i, j: (0, 0))],
        out_specs=[
            pl.BlockSpec((tg, GROUP_WIDTH), lambda i, j: (i * nt + j, 0)),
            pl.BlockSpec((1, hk, GLA_DV), lambda i, j: (i, 0, 0)),
        ],
        out_shape=[jax.ShapeDtypeStruct((m, GROUP_WIDTH), F32), jax.ShapeDtypeStruct((b, hk, GLA_DV), F32)],
        scratch_shapes=[pltpu.VMEM((hk, GLA_DV), F32)],
        compiler_params=_cparams("parallel", "arbitrary"),
        name="gla_block",
    )(gla, loga, s0, *consts, gnorm.reshape(1, GLA_DV))
    return out, sfin


CAUSAL_STEP = 512
KEY_CHUNK = 2048
NEG_INF = float("-inf")
NT_DIMS = (((1,), (1,)), ((), ()))


def _dot_nt(a, b):
    return lax.dot_general(a, b, NT_DIMS, preferred_element_type=F32)


def _dot3(a, b):
    hi = a.astype(BF16)
    r1 = a - hi.astype(F32)
    mid = r1.astype(BF16)
    lo = (r1 - mid.astype(F32)).astype(BF16)
    return (jnp.dot(hi, b, preferred_element_type=F32) + jnp.dot(mid, b, preferred_element_type=F32)
            + jnp.dot(lo, b, preferred_element_type=F32))


MASK_BIAS = -(2.0 ** 100)


def _softmax_parts(neg):
    m = functools.reduce(jnp.maximum, [jnp.max(s, axis=-1, keepdims=True) for s in neg])
    m = jnp.where(m == NEG_INF, 0.0, m)
    es = [jnp.exp(s - m) for s in neg]
    tot = functools.reduce(jnp.add, [jnp.sum(e, axis=-1, keepdims=True) for e in es])
    return es, 1.0 / jnp.maximum(tot, 1e-30)


def _topk_select(score, jidx, k, lanes):
    if len(lanes) <= 4 * k:
        lane = lax.broadcasted_iota(jnp.int32, score.shape, 1)
        cnt = jnp.zeros(score.shape, F32)
        for li in lanes:
            col = score[:, li:li + 1]
            ge, gt = jnp.where(col >= score, 1.0, 0.0), jnp.where(col > score, 1.0, 0.0)
            cnt = cnt + jnp.where(lane > li, ge, gt)
        return jnp.where(score > NEG_INF, jnp.where(cnt < k, 1.0, 0.0), 0.0)
    big = jnp.float32(1e9)
    work, sel = score, jnp.zeros(score.shape, F32)
    for _ in range(k):
        m = jnp.max(work, axis=-1, keepdims=True)
        idx = jnp.min(jnp.where(work == m, jidx, big), axis=-1, keepdims=True)
        pick = jidx == idx
        sel = jnp.where(pick, 1.0, sel)
        work = jnp.where(pick, NEG_INF, work)
    return jnp.where(score > NEG_INF, sel, 0.0)


def _block_lanes(n_blocks, per_chunk):
    return [(j // per_chunk) * 128 + j % per_chunk for j in range(n_blocks)]


def _stack_heads(x, pad_to=None):
    parts = [x[:, h * HEAD_DIM:(h + 1) * HEAD_DIM] for h in range(x.shape[1] // HEAD_DIM)]
    out = jnp.concatenate(parts, axis=0)
    if pad_to is not None:
        out = jnp.concatenate([out, jnp.zeros((out.shape[0], pad_to - HEAD_DIM), out.dtype)], axis=1)
    return out


def _block_expand_matrix(block):
    return jnp.asarray(np.arange(KEY_CHUNK)[None, :] // block == np.arange(128)[:, None], dtype=BF16)


def _page_specs(n_pages, feats, layer):
    def idx(b, j, pt, *, k):
        return (layer, pt[b, j * n_pages + k], 0, 0)
    return [pl.BlockSpec((None, None, feats, PAGE_SIZE), functools.partial(idx, k=k)) for k in range(n_pages)]


def _nsa_prep_kernel(*refs, n_src, paged):
    if paged:
        refs = refs[1:]
    srcs, wc_ref, slc_ref, p_ref, xs_ref = refs[:n_src], refs[n_src], refs[n_src + 1], refs[n_src + 2], refs[n_src + 3]
    j = pl.program_id(1)
    rows = PAGE_SIZE if paged else srcs[0].shape[0]
    if paged:
        eye = (lax.broadcasted_iota(jnp.int32, (rows, rows), 0) == lax.broadcasted_iota(jnp.int32, (rows, rows), 1))
        eye = jnp.where(eye, 1.0, 0.0).astype(BF16)
    for k, src in enumerate(srcs):
        r0 = pl.multiple_of((j * n_src + k) * rows, rows)
        if paged:
            xs_ref[pl.ds(r0, rows), :] = _dot_nt(eye, src[0:128, :].astype(BF16))
            slc_ref[0, :, k * rows:(k + 1) * rows] = src[128:256, :].astype(BF16)
        else:
            xs_ref[pl.ds(r0, rows), :] = src[:, 0:128]
            slc_ref[0, k * rows:(k + 1) * rows, :] = src[:, 128:256].astype(BF16)

    @pl.when(j == pl.num_programs(1) - 1)
    def _():
        n = xs_ref.shape[0] // CMP_STRIDE
        acc = jnp.zeros((n, 512), F32)
        for r in range(CMP_STRIDE):
            x = xs_ref[pl.ds(r, n, stride=CMP_STRIDE), :].astype(BF16)
            acc = acc + jnp.dot(x, wc_ref[r], preferred_element_type=F32)
        p_ref[0] = acc


def _nsa_cmp_weights(w1):
    w = w1.reshape(2, 2, CMP_STRIDE, HEAD_DIM, 128)
    z = jnp.zeros((CMP_STRIDE, HEAD_DIM, 128), w1.dtype)
    top = jnp.concatenate([w[0, 0], z, w[0, 1], z], axis=-1)
    bot = jnp.concatenate([z, w[1, 0], z, w[1, 1]], axis=-1)
    return jnp.concatenate([top, bot], axis=1).astype(BF16)


def nsa_prep(wc, rows=None, cache=None, layer=None, page_table=None):
    if cache is None:
        b, t, _ = rows.shape
        kern = functools.partial(_nsa_prep_kernel, n_src=1, paged=False)
        return pl.pallas_call(
            kern, grid=(b, 1),
            in_specs=[pl.BlockSpec((None, t, 256), lambda i, j: (i, 0, 0)), pl.BlockSpec(wc.shape, lambda i, j: (0, 0, 0))],
            out_specs=[pl.BlockSpec((1, t, 128), lambda i, j: (i, 0, 0)),
                       pl.BlockSpec((1, t // CMP_STRIDE, 512), lambda i, j: (i, 0, 0))],
            out_shape=[jax.ShapeDtypeStruct((b, t, 128), BF16), jax.ShapeDtypeStruct((b, t // CMP_STRIDE, 512), F32)],
            scratch_shapes=[pltpu.VMEM((t, 128), F32)],
            compiler_params=_cparams("parallel", "arbitrary"), name="nsa_prep",
        )(rows, wc)
    b, n_pages = page_table.shape
    L = n_pages * PAGE_SIZE
    npg = min(32, n_pages)
    assert n_pages % npg == 0
    kern = functools.partial(_nsa_prep_kernel, n_src=npg, paged=True)
    grid_spec = pltpu.PrefetchScalarGridSpec(
        num_scalar_prefetch=1, grid=(b, n_pages // npg),
        in_specs=_page_specs(npg, 256, layer) + [pl.BlockSpec(wc.shape, lambda i, j, pt: (0, 0, 0))],
        out_specs=[pl.BlockSpec((1, 128, npg * PAGE_SIZE), lambda i, j, pt: (i, 0, j)),
                   pl.BlockSpec((1, L // CMP_STRIDE, 512), lambda i, j, pt: (i, 0, 0))],
        scratch_shapes=[pltpu.VMEM((L, 128), F32)])
    return pl.pallas_call(
        kern, grid_spec=grid_spec,
        out_shape=[jax.ShapeDtypeStruct((b, 128, L), BF16), jax.ShapeDtypeStruct((b, L // CMP_STRIDE, 512), F32)],
        compiler_params=_cparams("parallel", "arbitrary"), name="nsa_prep_paged",
    )(page_table, *([cache] * npg), wc)


def _nsa_attend_kernel(*refs, tq, pos0, n_cmp, n_sel, n_chunks, has_new, t_total):
    (qraw_ref, qrot_ref, gate_ref, p_ref, cvec_ref, w2_ref, ov_ref, e0_ref, slc_ref, win_ref) = refs[:10]
    if has_new:
        newrow_ref, newwin_ref = refs[10:12]
        o_ref, kcvc_ref, oslc_ref = refs[12:]
    else:
        o_ref, kcvc_ref, oslc_ref = refs[10:]
    qt = pl.program_id(1)
    scale = HEAD_DIM ** -0.5
    r = NSA_HEADS * tq
    ncp = p_ref.shape[1]

    @pl.when(qt == 0)
    def _():
        p = p_ref[0]
        hid = p[:, 0:256] + pltpu.roll(p[:, 256:512], ncp - 1, 0) + cvec_ref[...]
        kcvc_ref[...] = jnp.dot(jax.nn.gelu(hid).astype(BF16), w2_ref[...], preferred_element_type=F32).astype(BF16)

    q0 = qt * tq
    p_row = pos0 + q0 + lax.broadcasted_iota(jnp.int32, (r, 1), 0) % tq
    p_q = pos0 + q0 + lax.broadcasted_iota(jnp.int32, (tq, 1), 0)
    q_raw = (_stack_heads(qraw_ref[...], 128) * scale).astype(BF16)
    q_rot = (_stack_heads(qrot_ref[...], 128) * scale).astype(BF16)
    kcvc = kcvc_ref[...]

    n_idx = lax.broadcasted_iota(jnp.int32, (r, ncp), 1)
    last_visible = jnp.minimum(p_row - (CMP_BLOCK - 1), (n_cmp - 1) * CMP_STRIDE)
    (ec,), inv = _softmax_parts([jnp.where(n_idx * CMP_STRIDE <= last_visible, _dot_nt(q_raw, kcvc), NEG_INF)])
    pc = ec * inv
    o_cmp = jnp.dot(pc.astype(BF16), kcvc, preferred_element_type=F32)[:, HEAD_DIM:]
    imp_h = _dot3(pc, ov_ref[...])
    imp = functools.reduce(jnp.add, [imp_h[h * tq:(h + 1) * tq] for h in range(NSA_HEADS)])

    nsl = imp.shape[1]
    lane = lax.broadcasted_iota(jnp.int32, (tq, nsl), 1)
    jblk = (lane // 128) * (KEY_CHUNK // SEL_BLOCK) + lane % 128
    lane_ok = (lane % 128 < KEY_CHUNK // SEL_BLOCK) & (jblk < n_sel)
    cur = p_q // SEL_BLOCK
    forced = (jblk == 0) | (jblk == cur)
    score = jnp.where(forced, FORCE_SCORE, jnp.where(jblk < cur, imp, NEG_INF))
    score = jnp.where(lane_ok, score, NEG_INF)
    jf = jnp.where(lane_ok, jblk, 1 << 20).astype(F32)
    sel = _topk_select(score, jf, min(SEL_TOPK, n_sel), _block_lanes(n_sel, KEY_CHUNK // SEL_BLOCK))
    bias_r = jnp.concatenate([jnp.where(sel > 0.5, 0.0, MASK_BIAS)] * NSA_HEADS, axis=0).astype(BF16)

    klen = e0_ref.shape[1]

    def selected(ext):
        neg, pvs = [], []
        kl = lax.broadcasted_iota(jnp.int32, (r, ext), 1)
        for c in range(n_chunks):
            bias = jnp.dot(bias_r[:, c * 128:(c + 1) * 128], e0_ref[:, 0:ext], preferred_element_type=F32)
            if has_new:
                kv = slc_ref[0, :, c * klen:c * klen + ext]
                s = jnp.dot(q_rot, kv, preferred_element_type=F32)
                pvs.append(functools.partial(_dot_nt, b=kv))
            else:
                kv = slc_ref[0, c * klen:c * klen + ext, :]
                s = _dot_nt(q_rot, kv)
                pvs.append(functools.partial(lambda e, b: jnp.dot(e, b, preferred_element_type=F32), b=kv))
            neg.append(jnp.where(kl + c * klen <= p_row, s + bias, NEG_INF))
        if has_new:
            kv = newrow_ref[:, 128:256].astype(BF16)
            visible = pos0 + lax.broadcasted_iota(jnp.int32, (r, kv.shape[0]), 1) <= p_row
            neg.append(jnp.where(visible, _dot_nt(q_rot, kv), NEG_INF))
            pvs.append(functools.partial(lambda e, b: jnp.dot(e, b, preferred_element_type=F32), b=kv))
        es, inv = _softmax_parts(neg)
        o = functools.reduce(jnp.add, [pv(e.astype(BF16)) for e, pv in zip(es, pvs)])
        return (o * inv)[:, HEAD_DIM:]

    if has_new or n_chunks > 1 or klen % CAUSAL_STEP or klen == CAUSAL_STEP:
        o_slc = selected(klen)
    else:
        need = (q0 + tq + CAUSAL_STEP - 1) // CAUSAL_STEP
        for e in range(1, klen // CAUSAL_STEP + 1):
            @pl.when(need == e)
            def _(e=e):
                oslc_ref[...] = selected(e * CAUSAL_STEP)
        o_slc = oslc_ref[...]

    if has_new:
        wl = win_ref.shape[1]
        segs = [(win_ref[0].astype(BF16), pos0 - wl), (newwin_ref[...].astype(BF16), pos0)]
    else:
        span = min(t_total, WINDOW + tq)
        start = pl.multiple_of(jnp.clip(q0 - WINDOW, 0, t_total - span), tq)
        segs = [(win_ref[0, pl.ds(start, span), :].astype(BF16), pos0 + start)]
    neg = []
    for kv, kp0 in segs:
        diff = p_row - (kp0 + lax.broadcasted_iota(jnp.int32, (r, kv.shape[0]), 1))
        in_window = lax.bitcast_convert_type(diff, jnp.uint32) < WINDOW
        neg.append(jnp.where(in_window, _dot_nt(q_rot, kv), NEG_INF))
    es, inv = _softmax_parts(neg)
    o_win = functools.reduce(jnp.add, [jnp.dot(e.astype(BF16), kv, preferred_element_type=F32) for e, (kv, _) in zip(es, segs)])
    o_win = (o_win * inv)[:, HEAD_DIM:]

    g = gate_ref[...]
    outs = []
    for h in range(NSA_HEADS):
        rows = slice(h * tq, (h + 1) * tq)
        outs.append(g[:, h:h + 1] * o_cmp[rows] + g[:, 4 + h:5 + h] * o_slc[rows] + g[:, 8 + h:9 + h] * o_win[rows])
    o_ref[...] = jnp.concatenate(outs, axis=1)


def _nsa_overlap(ncp, n_cmp, n_sel, nsl):
    per = KEY_CHUNK // SEL_BLOCK
    ov = np.zeros((ncp, nsl), np.float32)
    cs = np.arange(n_cmp) * CMP_STRIDE
    js = np.arange(n_sel) * SEL_BLOCK
    hit = (cs[:, None] < js[None, :] + SEL_BLOCK) & (cs[:, None] + CMP_BLOCK > js[None, :])
    lanes = (np.arange(n_sel) // per) * 128 + np.arange(n_sel) % per
    ov[:n_cmp, lanes] = hit
    return jnp.asarray(ov, dtype=BF16)


def nsa_attend(nq_raw, nq_rot, gates, pproj, cvec, w2bd, slc, win, b, t, pos0, new_rows=None, new_win=None):
    has_new = new_rows is not None
    tq = _pick_tile(t, 128)
    nq = t // tq
    l_slc = slc.shape[2] if has_new else slc.shape[1]
    l_ctx = l_slc + (t if has_new else 0)
    assert l_slc % KEY_CHUNK == 0 or l_slc < KEY_CHUNK
    assert not has_new or (t <= SEL_BLOCK and l_slc % KEY_CHUNK == 0 and t < CMP_STRIDE and pos0 >= win.shape[1])
    n_chunks = max(l_slc // KEY_CHUNK, 1)
    n_cmp = (l_ctx - CMP_BLOCK) // CMP_STRIDE + 1
    n_sel = -(-l_ctx // SEL_BLOCK)
    per = KEY_CHUNK // SEL_BLOCK
    nsl = -(-n_sel // per) * 128
    ncp = pproj.shape[1]
    ov = _nsa_overlap(ncp, n_cmp, n_sel, nsl)
    e0 = _block_expand_matrix(SEL_BLOCK)
    if l_slc < KEY_CHUNK:
        e0 = e0[:, :l_slc]
    kern = functools.partial(_nsa_attend_kernel, tq=tq, pos0=pos0, n_cmp=n_cmp, n_sel=n_sel, n_chunks=n_chunks,
                             has_new=has_new, t_total=t)
    tile = lambda w: pl.BlockSpec((tq, w), lambda i, j: (i * nq + j, 0))
    const = lambda a: pl.BlockSpec(a.shape, lambda i, j: (0,) * a.ndim)
    batch = lambda a: pl.BlockSpec((1,) + a.shape[1:], lambda i, j: (i,) + (0,) * (a.ndim - 1))
    in_specs = [tile(256), tile(256), tile(128), batch(pproj), const(cvec), const(w2bd), const(ov), const(e0),
                batch(slc), batch(win)]
    args = [nq_raw, nq_rot, gates, pproj, cvec, w2bd, ov, e0, slc, win]
    if has_new:
        in_specs += [tile(256), tile(128)]
        args += [new_rows, new_win]
    return pl.pallas_call(
        kern, grid=(b, nq), in_specs=in_specs, out_specs=tile(256),
        out_shape=jax.ShapeDtypeStruct((b * t, GROUP_WIDTH), F32),
        scratch_shapes=[pltpu.VMEM((ncp, 128), BF16), pltpu.VMEM((NSA_HEADS * tq, HEAD_DIM), F32)],
        compiler_params=_cparams("parallel", "arbitrary"), name="nsa_attend",
    )(*args)


MOBA_PER_CHUNK = KEY_CHUNK // MOBA_BLOCK
MOBA_CHUNKS_PER_STEP = 2


def _moba_prep_kernel(*refs, n_src, paged):
    if paged:
        refs = refs[1:]
    srcs, avg_ref, kv_ref, mean_ref = refs[:n_src], refs[n_src], refs[n_src + 1], refs[n_src + 2]
    if paged:
        for k, src in enumerate(srcs):
            kv_ref[0, :, k * PAGE_SIZE:(k + 1) * PAGE_SIZE] = src[...].astype(BF16)
        mean_ref[0] = jnp.dot(kv_ref[0, 0:256, :], avg_ref[...], preferred_element_type=F32)
    else:
        kv_ref[0] = srcs[0][...].astype(BF16)
        mean_ref[0] = jnp.dot(avg_ref[...], kv_ref[0, :, 0:256], preferred_element_type=F32)


def moba_prep(rows=None, cache=None, layer=None, page_table=None):
    if cache is None:
        b, t, _ = rows.shape
        klen = min(KEY_CHUNK, t)
        n_chunks, n_src = t // klen, 1
        in_specs = [pl.BlockSpec((None, klen, 512), lambda i, j: (i, j, 0))]
        args, grid_kw, nsp = (rows,), {}, 0
    else:
        b, n_pages = page_table.shape
        klen = min(KEY_CHUNK, n_pages * PAGE_SIZE)
        spb = MOBA_CHUNKS_PER_STEP if n_pages * PAGE_SIZE % (MOBA_CHUNKS_PER_STEP * klen) == 0 else 1
        n_src = spb * klen // PAGE_SIZE
        n_chunks = n_pages // n_src
        in_specs = _page_specs(n_src, 512, layer)
        args, nsp = (page_table,) + (cache,) * n_src, 1
    avg = (_block_expand_matrix(MOBA_BLOCK)[:, :klen].astype(F32) / MOBA_BLOCK).astype(BF16)
    if nsp:
        avg = jnp.kron(jnp.eye(spb, dtype=BF16), avg.T)
        out_specs = [pl.BlockSpec((1, 512, spb * klen), lambda i, j, pt: (i, 0, j)),
                     pl.BlockSpec((1, 256, spb * 128), lambda i, j, pt: (i, 0, j))]
        out_shape = [jax.ShapeDtypeStruct((b, 512, n_chunks * spb * klen), BF16),
                     jax.ShapeDtypeStruct((b, 256, n_chunks * spb * 128), F32)]
        cmap = lambda i, j, pt: (0, 0)
    else:
        out_specs = [pl.BlockSpec((1, klen, 512), lambda i, j: (i, j, 0)),
                     pl.BlockSpec((1, 128, 256), lambda i, j: (i, j, 0))]
        out_shape = [jax.ShapeDtypeStruct((b, n_chunks * klen, 512), BF16),
                     jax.ShapeDtypeStruct((b, n_chunks * 128, 256), F32)]
        cmap = lambda i, j: (0, 0)
    grid_spec = pltpu.PrefetchScalarGridSpec(
        num_scalar_prefetch=nsp, grid=(b, n_chunks),
        in_specs=in_specs + [pl.BlockSpec(avg.shape, cmap)], out_specs=out_specs)
    return pl.pallas_call(
        functools.partial(_moba_prep_kernel, n_src=n_src, paged=bool(nsp)), grid_spec=grid_spec, out_shape=out_shape,
        compiler_params=_cparams("parallel", "arbitrary"), name="moba_prep",
    )(*args, avg)


def _moba_attend_kernel(*refs, tq, pos0, n_cols, n_chunks, spb, has_new):
    mq_ref, mean_ref, e0_ref, kv_ref = refs[:4]
    if has_new:
        new_ref = refs[4]
        o_ref, sel_ref, m_ref, l_ref, acc_ref = refs[5:]
    else:
        o_ref, sel_ref, m_ref, l_ref, acc_ref = refs[4:]
    qt, c = pl.program_id(1), pl.program_id(2)
    scale = HEAD_DIM ** -0.5
    r2 = 2 * tq
    klen = e0_ref.shape[1]
    q = mq_ref[...]
    q0 = pos0 + qt * tq
    p_row = q0 + lax.broadcasted_iota(jnp.int32, (r2, 1), 0) % tq

    def gate_scores(qh, means):
        return jnp.dot(qh, means, preferred_element_type=F32) if has_new else _dot_nt(qh, means)

    @pl.when(c == 0)
    def _():
        means = mean_ref[0].astype(BF16)
        nsl = n_chunks * 128
        lane = lax.broadcasted_iota(jnp.int32, (tq, nsl), 1)
        jblk = (lane // 128) * MOBA_PER_CHUNK + lane % 128
        lane_ok = (lane % 128 < MOBA_PER_CHUNK) & (jblk < n_cols)
        own = (q0 + lax.broadcasted_iota(jnp.int32, (tq, 1), 0)) // MOBA_BLOCK
        jf = jnp.where(lane_ok, jblk, 1 << 20).astype(F32)
        head_of_lane = lax.broadcasted_iota(jnp.int32, q.shape, 1) // HEAD_DIM
        for h in range(MOBA_HEADS):
            qh = jnp.where(head_of_lane == h, q, 0.0).astype(BF16)
            score = jnp.where(lane_ok, jnp.where(jblk < own, gate_scores(qh, means), NEG_INF), NEG_INF)
            sel = _topk_select(score, jf, MOBA_TOPK, _block_lanes(n_cols, MOBA_PER_CHUNK))
            bias = jnp.where(sel > 0.5, 0.0, jnp.where(jblk == own, 0.0, MASK_BIAS))
            for cc in range(n_chunks):
                sel_ref[cc, h * tq:(h + 1) * tq, :] = bias[:, cc * 128:(cc + 1) * 128]
        m_ref[...] = jnp.full(m_ref.shape, NEG_INF, F32)
        l_ref[...] = jnp.zeros(l_ref.shape, F32)
        acc_ref[...] = jnp.zeros(acc_ref.shape, F32)

    lane128 = lax.broadcasted_iota(jnp.int32, (tq, 128), 1)

    def pair_q(pr):
        x = q[:, 128 * pr:128 * (pr + 1)] * scale
        return jnp.concatenate([jnp.where(lane128 < HEAD_DIM, x, 0.0), jnp.where(lane128 >= HEAD_DIM, x, 0.0)],
                               axis=0).astype(BF16)

    def update(pr, neg, v, v_feature_major=False):
        m_old = m_ref[pr][:, 0:1]
        m_new = jnp.maximum(m_old, jnp.max(neg, axis=-1, keepdims=True))
        m_safe = jnp.where(m_new == NEG_INF, 0.0, m_new)
        alpha = jnp.exp(m_old - m_safe)
        e = jnp.exp(neg - m_safe)
        pv = _dot_nt(e.astype(BF16), v) if v_feature_major else jnp.dot(e.astype(BF16), v, preferred_element_type=F32)
        l_ref[pr] = alpha * l_ref[pr] + jnp.sum(e, axis=-1, keepdims=True)
        acc_ref[pr] = alpha * acc_ref[pr] + pv
        m_ref[pr] = jnp.broadcast_to(m_new, (r2, 128))

    own_row = p_row // MOBA_BLOCK

    def attend(ext, sub=0):
        cc = c * spb + sub
        kpos = cc * klen + lax.broadcasted_iota(jnp.int32, (r2, ext), 1)
        for pr in range(2):
            bias = jnp.dot(sel_ref[cc, 2 * pr * tq:(2 * pr + 2) * tq, :].astype(BF16), e0_ref[:, 0:ext],
                           preferred_element_type=F32)
            if has_new:
                keys = slice(sub * klen, sub * klen + ext)
                s = jnp.dot(pair_q(pr), kv_ref[0, 128 * pr:128 * (pr + 1), keys], preferred_element_type=F32)
                neg = jnp.where(kpos <= p_row, s + bias, NEG_INF)
                update(pr, neg, kv_ref[0, 256 + 128 * pr:256 + 128 * (pr + 1), keys], v_feature_major=True)
            else:
                s = _dot_nt(pair_q(pr), kv_ref[0, 0:ext, 128 * pr:128 * (pr + 1)])
                neg = jnp.where(kpos <= p_row, s + bias, NEG_INF)
                update(pr, neg, kv_ref[0, 0:ext, 256 + 128 * pr:256 + 128 * (pr + 1)])

    if has_new or n_chunks > 1 or klen % CAUSAL_STEP or klen == CAUSAL_STEP:
        for sub in range(spb):
            attend(klen, sub)
    else:
        need = (qt * tq + tq + CAUSAL_STEP - 1) // CAUSAL_STEP
        for e in range(1, klen // CAUSAL_STEP + 1):
            @pl.when(need == e)
            def _(e=e):
                attend(e * CAUSAL_STEP)

    @pl.when(c == n_chunks // spb - 1)
    def _():
        outs = []
        for pr in range(2):
            if has_new:
                t_new = new_ref.shape[0]
                kp = pos0 + lax.broadcasted_iota(jnp.int32, (r2, t_new), 1)
                s = _dot_nt(pair_q(pr), new_ref[:, 128 * pr:128 * (pr + 1)].astype(BF16))
                neg = jnp.where(kp <= p_row, jnp.where(kp // MOBA_BLOCK == own_row, s, NEG_INF), NEG_INF)
                update(pr, neg, new_ref[:, 256 + 128 * pr:256 + 128 * (pr + 1)].astype(BF16))
            o = acc_ref[pr] / jnp.maximum(l_ref[pr][:, 0:1], 1e-30)
            outs += [o[:tq, :HEAD_DIM], o[tq:, HEAD_DIM:]]
        o_ref[...] = jnp.concatenate(outs, axis=1)


def moba_attend(mq, means, kv, b, t, pos0, new_rows=None):
    has_new = new_rows is not None
    tq = _pick_tile(t, 128)
    nq = t // tq
    l_kv = kv.shape[2] if has_new else kv.shape[1]
    klen = min(KEY_CHUNK, l_kv)
    n_chunks = l_kv // klen
    n_full = (l_kv + (t if has_new else 0)) // MOBA_BLOCK
    assert n_full >= MOBA_TOPK and l_kv % klen == 0 and klen % MOBA_BLOCK == 0
    assert not has_new or t <= MOBA_BLOCK
    e0 = _block_expand_matrix(MOBA_BLOCK)[:, :klen]
    spb = MOBA_CHUNKS_PER_STEP if has_new and n_chunks % MOBA_CHUNKS_PER_STEP == 0 else 1
    kern = functools.partial(_moba_attend_kernel, tq=tq, pos0=pos0, n_cols=n_full, n_chunks=n_chunks, spb=spb,
                             has_new=has_new)
    tile = lambda w: pl.BlockSpec((tq, w), lambda i, j, c: (i * nq + j, 0))
    kv_spec = (pl.BlockSpec((1, 512, spb * klen), lambda i, j, c: (i, 0, c)) if has_new
               else pl.BlockSpec((1, klen, 512), lambda i, j, c: (i, c, 0)))
    in_specs = [tile(256), pl.BlockSpec((1,) + means.shape[1:], lambda i, j, c: (i, 0, 0)),
                pl.BlockSpec(e0.shape, lambda i, j, c: (0, 0)), kv_spec]
    args = [mq, means, e0, kv]
    if has_new:
        in_specs.append(tile(512))
        args.append(new_rows)
    return pl.pallas_call(
        kern, grid=(b, nq, n_chunks // spb), in_specs=in_specs, out_specs=tile(256),
        out_shape=jax.ShapeDtypeStruct((b * t, GROUP_WIDTH), F32),
        scratch_shapes=[pltpu.VMEM((n_chunks, MOBA_HEADS * tq, 128), F32), pltpu.VMEM((2, 2 * tq, 128), F32),
                        pltpu.VMEM((2, 2 * tq, 128), F32), pltpu.VMEM((2, 2 * tq, 128), F32)],
        compiler_params=_cparams("parallel", "arbitrary", "arbitrary"), name="moba_attend",
    )(*args)


def _layer(h, p_emb, b, t, pos0, lw, past, final_gain, final):
    prompt = past is None
    pos = pos0 + jnp.arange(t, dtype=jnp.int32)
    h = ffn_block(h, lw['norm_ffn1'], lw['ffn1_w_in'], lw['ffn1_w_out'])
    (gla, loga, nq_raw, nq_rot, nsa_rows, win_rows, gates, su, mq, moba_rows) = mixin_block(
        h, lw['norm_mix'], lw['wcat'], _rope_table(pos), lw['gla_gate_w'], lw['gla_gate_b'], t)

    s0 = jnp.zeros((b, GLA_HEADS * GLA_DK, GLA_DV), F32) if prompt else past['gla'].reshape(b, -1, GLA_DV)
    o_gla, gla_state = gla_block(gla, loga, s0, lw['gla_norm'], b, GLA_CHUNK if prompt else t)
    gla_state = gla_state.reshape(b, GLA_HEADS, GLA_DK, GLA_DV)

    rows3 = nsa_rows.reshape(b, t, 4 * HEAD_DIM)
    win3 = win_rows.reshape(b, t, 2 * HEAD_DIM)
    if prompt:
        slc, pproj = nsa_prep(lw['nsa_wc'], rows=rows3)
        o_nsa = nsa_attend(nq_raw, nq_rot, gates, pproj, lw['nsa_cvec'], lw['nsa_w2bd'], slc, win3, b, t, 0)
        new_win = win3[:, -min(WINDOW, t):]
    else:
        slc, pproj = nsa_prep(lw['nsa_wc'], cache=past['nsa_cache'], layer=past['layer'], page_table=past['page_table'])
        wb = past['win'].reshape(b, -1, 2 * HEAD_DIM)
        o_nsa = nsa_attend(nq_raw, nq_rot, gates, pproj, lw['nsa_cvec'], lw['nsa_w2bd'], slc, wb, b, t, pos0,
                           new_rows=nsa_rows, new_win=win_rows)
        new_win = jnp.concatenate([wb, win3], axis=1)[:, -wb.shape[1]:]

    if prompt:
        h0 = jnp.zeros((b, 1, 2 * S5_LANES), F32)
    else:
        h0 = jnp.concatenate([past['s5_re'].reshape(b, 1, -1), past['s5_im'].reshape(b, 1, -1)], axis=-1)
    o_s5, s5_re, s5_im = s5_block(su, h0, lw['s5_pw'], lw['s5_bcat'], lw['s5_ccat'], lw['s5_d'],
                                  lw['s5_glu_w'], lw['s5_glu_b'], b)

    mrows = moba_rows.reshape(b, t, 2, MOBA_HEADS, HEAD_DIM)
    if prompt:
        mkv, means = moba_prep(rows=moba_rows.reshape(b, t, 512))
        o_moba = moba_attend(mq, means, mkv, b, t, 0)
    else:
        mkv, means = moba_prep(cache=past['moba_cache'], layer=past['layer'], page_table=past['page_table'])
        o_moba = moba_attend(mq, means, mkv, b, t, pos0, new_rows=moba_rows)

    h = mixout_block(h, (o_gla, o_nsa, o_s5, o_moba), lw['w_mix_out'])
    h = ffn_block(h, lw['norm_ffn2'], lw['ffn2_w_in'], lw['ffn2_w_out'])
    h = ple_block(h, p_emb, lw['norm_ple'], lw['ple_w'], lw['ple_gate_w'], final_gain, final)
    state = {'nsa': nsa_rows.reshape(b, t, 4, HEAD_DIM), 'moba': mrows,
             'win': new_win.reshape(b, -1, 2, HEAD_DIM), 'gla': gla_state, 's5_re': s5_re, 's5_im': s5_im}
    return h, state


def kernel(x_prompt, x_sample, p_prompt, p_sample, cache_nsa_kv, cache_moba_kv, state_nsa_win, state_gla,
           state_s5_re, state_s5_im, page_table, norm_ffn1, ffn1_w_in, ffn1_w_out, norm_mix, w_mix_in, w_mix_out,
           gla_gate_w, gla_gate_b, gla_norm, nsa_cmp_pe, nsa_cmp_w1, nsa_cmp_w2, s5_a_re, s5_a_im, s5_b_re, s5_b_im,
           s5_c_re, s5_c_im, s5_d, s5_log_dt, s5_glu_w, s5_glu_b, norm_ffn2, ffn2_w_in, ffn2_w_out, norm_ple, ple_w,
           ple_gate_w, norm_final):
    bp, tp, d = x_prompt.shape
    bs, ts, _ = x_sample.shape
    depth = p_prompt.shape[0]
    past_len = page_table.shape[1] * PAGE_SIZE
    params = {'norm_ffn1': norm_ffn1, 'norm_mix': norm_mix, 'gla_gate_w': gla_gate_w, 'gla_gate_b': gla_gate_b,
              'gla_norm': gla_norm, 'nsa_cmp_pe': nsa_cmp_pe, 'nsa_cmp_w1': nsa_cmp_w1, 'nsa_cmp_w2': nsa_cmp_w2,
              's5_a_re': s5_a_re, 's5_a_im': s5_a_im, 's5_b_re': s5_b_re, 's5_b_im': s5_b_im, 's5_c_re': s5_c_re,
              's5_c_im': s5_c_im, 's5_d': s5_d, 's5_log_dt': s5_log_dt, 's5_glu_w': s5_glu_w, 's5_glu_b': s5_glu_b,
              'norm_ffn2': norm_ffn2, 'norm_ple': norm_ple}
    hp = x_prompt.reshape(bp * tp, d)
    hs = x_sample.reshape(bs * ts, d)
    nsa_cache = jnp.transpose(cache_nsa_kv, (0, 1, 3, 4, 2)).reshape(cache_nsa_kv.shape[:2] + (4 * HEAD_DIM, PAGE_SIZE))
    moba_cache = jnp.transpose(cache_moba_kv, (0, 1, 3, 4, 5, 2)).reshape(
        cache_moba_kv.shape[:2] + (2 * MOBA_HEADS * HEAD_DIM, PAGE_SIZE))
    st_p, st_s = [], []
    for i in range(depth):
        lw = {k: v[i] for k, v in params.items()}
        lw['wcat'] = _mixin_weights(w_mix_in[i])
        for name, w in (('ffn1_w_in', ffn1_w_in), ('ffn1_w_out', ffn1_w_out), ('ffn2_w_in', ffn2_w_in),
                        ('ffn2_w_out', ffn2_w_out), ('w_mix_out', w_mix_out), ('ple_w', ple_w),
                        ('ple_gate_w', ple_gate_w)):
            lw[name] = w[i].astype(BF16)
        lw['s5_pw'], lw['s5_bcat'], lw['s5_ccat'] = _s5_params(s5_a_re[i], s5_a_im[i], s5_b_re[i], s5_b_im[i],
                                                               s5_c_re[i], s5_c_im[i], s5_log_dt[i])
        lw['s5_glu_w'] = s5_glu_w[i].astype(BF16)
        lw['nsa_wc'] = _nsa_cmp_weights(nsa_cmp_w1[i])
        pe, w1, w2 = nsa_cmp_pe[i], nsa_cmp_w1[i], nsa_cmp_w2[i]
        lw['nsa_cvec'] = jnp.concatenate([jnp.dot(pe[s].reshape(1, -1), w1[s], precision=HIGHEST) for s in range(2)], axis=1)
        zw = jnp.zeros_like(w2[0])
        lw['nsa_w2bd'] = jnp.concatenate([jnp.concatenate([w2[0], zw], axis=1),
                                          jnp.concatenate([zw, w2[1]], axis=1)], axis=0).astype(BF16)
        final = i == depth - 1
        hp, sp = _layer(hp, p_prompt[i].reshape(bp * tp, -1), bp, tp, 0, lw, None, norm_final, final)
        past = {'nsa_cache': nsa_cache, 'moba_cache': moba_cache, 'layer': i, 'page_table': page_table,
                'win': state_nsa_win[i], 'gla': state_gla[i], 's5_re': state_s5_re[i], 's5_im': state_s5_im[i]}
        hs, ss = _layer(hs, p_sample[i].reshape(bs * ts, -1), bs, ts, past_len, lw, past, norm_final, final)
        st_p.append(sp)
        st_s.append(ss)

    def stack(states, name):
        return jnp.stack([s[name] for s in states], axis=0)

    return (hp.reshape(bp, tp, d), hs.reshape(bs, ts, d),
            stack(st_p, 'nsa'), stack(st_s, 'nsa'), stack(st_p, 'moba'), stack(st_s, 'moba'),
            stack(st_p, 'win'), stack(st_s, 'win'), stack(st_p, 'gla'), stack(st_s, 'gla'),
            stack(st_p, 's5_re'), stack(st_s, 's5_re'), stack(st_p, 's5_im'), stack(st_s, 's5_im'))
```
